```python
import jax, jax.numpy as jnp
from jax import lax
import numpy as np

D_MODEL = 2048
BATCH = 8
SEQ = 4096
DEPTH = 2
DEC_BATCH = 8
DEC_SEQ = 32
PAST_LEN = 2048

CHUNK = 64
EPS = 1e-6
GDN_HEADS = D_MODEL // 256
GDN_DK = 128
GDN_DV = 128
GDN_QK = GDN_HEADS * GDN_DK
GDN_VW = GDN_HEADS * GDN_DV
CONV_WIDTH = 4
CONV_DIM = 2 * GDN_QK + GDN_VW
ATT_HEADS = D_MODEL // 256
ATT_HD = 128
ATT_W = ATT_HEADS * ATT_HD
BAND_PREV = 8
BAND = (BAND_PREV + 1) * CHUNK
REL_CLIP = 128
N_REL = 2 * REL_CLIP + 1
IN_DIM = CONV_DIM + GDN_VW + 2 * GDN_HEADS + 4 * ATT_W + 2 * D_MODEL

kernel_name = 'hybrid_stream_gdn_bandattn_step'


def _split_points():
    sizes = (CONV_DIM, GDN_VW, GDN_HEADS, GDN_HEADS, ATT_W, ATT_W, ATT_W, ATT_W, D_MODEL, D_MODEL)
    return [int(s) for s in np.cumsum(sizes)[:-1]]


def rms_norm(x, w):
    x32 = x.astype(jnp.float32)
    y = x32 * lax.rsqrt(jnp.mean(x32 * x32, axis=-1, keepdims=True) + EPS)
    return (y * w.astype(jnp.float32)).astype(x.dtype)


def l2_normalize(x):
    x32 = x.astype(jnp.float32)
    return x32 * lax.rsqrt(jnp.sum(x32 * x32, axis=-1, keepdims=True) + EPS)


def gated_delta_rule(q, k, v, g, beta, s0):
    Bsz, T, H, DK = q.shape
    DV = v.shape[-1]
    C = min(CHUNK, T)
    n = T // C

    def blocks(t):
        t = t.astype(jnp.float32).reshape((Bsz, n, C, H) + t.shape[3:])
        return jnp.moveaxis(t, (1, 3), (0, 2))

    q, k, v, g, beta = blocks(q), blocks(k), blocks(v), blocks(g), blocks(beta)
    G = jnp.cumsum(g, axis=-1)
    idx = jnp.arange(C)
    incl = idx[:, None] >= idx[None, :]
    strict = idx[:, None] > idx[None, :]
    diff = G[..., :, None] - G[..., None, :]
    dmat = jnp.where(incl, jnp.exp(jnp.where(incl, diff, 0.0)), 0.0)
    kb = k * beta[..., None]
    m = jnp.where(strict, jnp.einsum('nbhid,nbhjd->nbhij', kb, k) * dmat, 0.0)
    eye = jnp.eye(C, dtype=jnp.float32)
    t_inv = lax.linalg.triangular_solve(m + eye, jnp.broadcast_to(eye, m.shape),
                                        left_side=True, lower=True)
    u = jnp.einsum('nbhij,nbhjd->nbhid', t_inv, v * beta[..., None])
    w = jnp.einsum('nbhij,nbhjd->nbhid', t_inv, kb * jnp.exp(G)[..., None])
    a_intra = jnp.einsum('nbhid,nbhjd->nbhij', q, k) * dmat
    qg = q * jnp.exp(G)[..., None]
    kg = k * jnp.exp(G[..., -1:] - G)[..., None]
    g_last = jnp.exp(G[..., -1])

    def step(s, xs):
        qg_c, kg_c, u_c, w_c, a_c, gl = xs
        v_new = u_c - jnp.einsum('bhid,bhde->bhie', w_c, s)
        o = jnp.einsum('bhid,bhde->bhie', qg_c, s) + jnp.einsum('bhij,bhje->bhie', a_c, v_new)
        s = s * gl[..., None, None] + jnp.einsum('bhid,bhie->bhde', kg_c, v_new)
        return s, o

    s_fin, o = lax.scan(step, s0.astype(jnp.float32), (qg, kg, u, w, a_intra, g_last))
    o = jnp.moveaxis(o, (0, 2), (1, 3)).reshape(Bsz, T, H, DV)
    return o, s_fin


def gdn_branch(qkv, gate, b_raw, a_raw, conv_buf, s0, conv_w, a_log, dt_bias, norm_w):
    Bsz, T, _ = qkv.shape
    full = jnp.concatenate([conv_buf.astype(qkv.dtype), qkv], axis=1)
    conv = full[:, 0:T] * conv_w[0]
    for i in range(1, CONV_WIDTH):
        conv = conv + full[:, i:i + T] * conv_w[i]
    new_buf = full[:, full.shape[1] - (CONV_WIDTH - 1):]
    act = jax.nn.silu(conv)
    q, k, v = jnp.split(act, [GDN_QK, 2 * GDN_QK], axis=-1)
    q = l2_normalize(q.reshape(Bsz, T, GDN_HEADS, GDN_DK)) * (GDN_DK ** -0.5)
    k = l2_normalize(k.reshape(Bsz, T, GDN_HEADS, GDN_DK))
    v = v.reshape(Bsz, T, GDN_HEADS, GDN_DV)
    beta = jax.nn.sigmoid(b_raw.astype(jnp.float32))
    g = -jnp.exp(a_log.astype(jnp.float32)) * jax.nn.softplus(
        a_raw.astype(jnp.float32) + dt_bias.astype(jnp.float32))
    o, s_new = gated_delta_rule(q, k, v, g, beta, s0)
    o = rms_norm(o, norm_w).reshape(Bsz, T, GDN_VW).astype(qkv.dtype) * jax.nn.silu(gate)
    return o, new_buf, s_new.astype(s0.dtype)


def band_attention(q, k, v, q_pos, k_pos, rel_bias):
    s = jnp.einsum('bqhd,bkhd->bhqk', q, k).astype(jnp.float32) * (ATT_HD ** -0.5)
    dist = jnp.clip(q_pos[:, None] - k_pos[None, :], -REL_CLIP, REL_CLIP) + REL_CLIP
    s = s + rel_bias[:, dist].astype(jnp.float32)[None]
    qc = q_pos // CHUNK
    kc = k_pos // CHUNK
    valid = (k_pos[None, :] >= 0) & (kc[None, :] <= qc[:, None]) & (kc[None, :] >= qc[:, None] - BAND_PREV)
    s = jnp.where(valid[None, None], s, -1e30)
    p = jax.nn.softmax(s, axis=-1)
    return jnp.einsum('bhqk,bkhd->bqhd', p.astype(v.dtype), v)


def band_attention_prompt(q, k, v, rel_bias):
    Bsz, T, H, D = q.shape
    pad = BAND_PREV * CHUNK
    k_pad = jnp.pad(k, ((0, 0), (pad, 0), (0, 0), (0, 0)))
    v_pad = jnp.pad(v, ((0, 0), (pad, 0), (0, 0), (0, 0)))

    def one_chunk(ci):
        start = ci * CHUNK
        q_c = lax.dynamic_slice_in_dim(q, start, CHUNK, axis=1)
        k_c = lax.dynamic_slice_in_dim(k_pad, start, BAND, axis=1)
        v_c = lax.dynamic_slice_in_dim(v_pad, start, BAND, axis=1)
        q_pos = start + jnp.arange(CHUNK)
        k_pos = start - pad + jnp.arange(BAND)
        return band_attention(q_c, k_c, v_c, q_pos, k_pos, rel_bias)

    o = lax.map(one_chunk, jnp.arange(T // CHUNK))
    return jnp.moveaxis(o, 0, 1).reshape(Bsz, T, H, D)


def layer(x, c, conv_buf, s0, k_cache, v_cache, p, prompt):
    (norm_w, ada_w, ada_b, w_in, conv_w, a_log, dt_bias, gdn_norm_w,
     q_norm_w, k_norm_w, rel_bias, w_proj_a, w_proj_b, w_out) = p
    Bsz, T, _ = x.shape
    mod = jax.nn.silu(c) @ ada_w + ada_b
    shift, scale, gate = jnp.split(mod, 3, axis=-1)
    h = rms_norm(x, norm_w) * (1 + scale[:, None, :]) + shift[:, None, :]
    z = h @ w_in
    (qkv_a, gate_a, b_a, a_a, q_b, k_b, v_b, gate_b, m_a, m_b) = jnp.split(z, _split_points(), axis=-1)
    o_a, new_buf, s_new = gdn_branch(qkv_a, gate_a, b_a, a_a, conv_buf, s0, conv_w,
                                     a_log, dt_bias, gdn_norm_w)
    q = rms_norm(q_b.reshape(Bsz, T, ATT_HEADS, ATT_HD), q_norm_w)
    k = rms_norm(k_b.reshape(Bsz, T, ATT_HEADS, ATT_HD), k_norm_w)
    v = v_b.reshape(Bsz, T, ATT_HEADS, ATT_HD)
    if prompt:
        o_b = band_attention_prompt(q, k, v, rel_bias)
        keep = min(BAND_PREV * CHUNK, T)
        k_new = k[:, T - keep:]
        v_new = v[:, T - keep:]
    else:
        n_cached = k_cache.shape[1]
        k_all = jnp.concatenate([k_cache.astype(k.dtype), k], axis=1)
        v_all = jnp.concatenate([v_cache.astype(v.dtype), v], axis=1)
        q_pos = PAST_LEN + jnp.arange(T)
        k_pos = PAST_LEN - n_cached + jnp.arange(n_cached + T)
        o_b = band_attention(q, k_all, v_all, q_pos, k_pos, rel_bias)
        k_new = k
        v_new = v
    o_b = o_b.reshape(Bsz, T, ATT_W) * jax.nn.silu(gate_b)
    merged = jax.nn.sigmoid(m_a) * (o_a @ w_proj_a) + jax.nn.sigmoid(m_b) * (o_b @ w_proj_b)
    y = x + gate[:, None, :] * (merged @ w_out)
    return y, new_buf, s_new, k_new, v_new


def setup_inputs(seed: int = 0) -> dict:
    key = jax.random.key(seed)
    ks = jax.random.split(key, 24)
    f32 = jnp.float32
    n_cached = min(BAND_PREV * CHUNK, PAST_LEN)
    nrm = lambda k, shape, s: jax.random.normal(k, shape, f32) * s
    dt = jnp.exp(jax.random.uniform(ks[12], (DEPTH, GDN_HEADS), f32, np.log(1e-3), np.log(1e-1)))
    return {
        'x_prompt': nrm(ks[0], (BATCH, SEQ, D_MODEL), 1.0),
        'x_sample': nrm(ks[1], (DEC_BATCH, DEC_SEQ, D_MODEL), 1.0),
        'c_prompt': nrm(ks[2], (BATCH, D_MODEL), 1.0),
        'c_sample': nrm(ks[3], (DEC_BATCH, D_MODEL), 1.0),
        'state_conv': nrm(ks[4], (DEPTH, DEC_BATCH, CONV_WIDTH - 1, CONV_DIM), 1.0),
        'state_delta': nrm(ks[5], (DEPTH, DEC_BATCH, GDN_HEADS, GDN_DK, GDN_DV), 0.1),
        'cache_k': nrm(ks[6], (DEPTH, DEC_BATCH, n_cached, ATT_HEADS, ATT_HD), 1.0),
        'cache_v': nrm(ks[7], (DEPTH, DEC_BATCH, n_cached, ATT_HEADS, ATT_HD), 1.0),
        'norm_w': 1.0 + nrm(ks[8], (DEPTH, D_MODEL), 0.02),
        'ada_w': nrm(ks[9], (DEPTH, D_MODEL, 3 * D_MODEL), 0.5 * D_MODEL ** -0.5),
        'ada_b': nrm(ks[10], (DEPTH, 3 * D_MODEL), 0.02),
        'w_in': nrm(ks[11], (DEPTH, D_MODEL, IN_DIM), D_MODEL ** -0.5),
        'conv_w': nrm(ks[13], (DEPTH, CONV_WIDTH, CONV_DIM), 0.5),
        'gdn_a_log': jnp.log(jax.random.uniform(ks[14], (DEPTH, GDN_HEADS), f32, 1.0, 16.0)),
        'gdn_dt_bias': dt + jnp.log(-jnp.expm1(-dt)),
        'gdn_norm_w': 1.0 + nrm(ks[15], (DEPTH, GDN_DV), 0.02),
        'q_norm_w': 1.0 + nrm(ks[16], (DEPTH, ATT_HD), 0.02),
        'k_norm_w': 1.0 + nrm(ks[17], (DEPTH, ATT_HD), 0.02),
        'rel_bias': nrm(ks[18], (DEPTH, ATT_HEADS, N_REL), 0.3),
        'w_proj_a': nrm(ks[19], (DEPTH, GDN_VW, D_MODEL), GDN_VW ** -0.5),
        'w_proj_b': nrm(ks[20], (DEPTH, ATT_W, D_MODEL), ATT_W ** -0.5),
        'w_out': nrm(ks[21], (DEPTH, D_MODEL, D_MODEL), D_MODEL ** -0.5),
    }


def reference(x_prompt, x_sample, c_prompt, c_sample, state_conv, state_delta, cache_k, cache_v,
              norm_w, ada_w, ada_b, w_in, conv_w, gdn_a_log, gdn_dt_bias, gdn_norm_w,
              q_norm_w, k_norm_w, rel_bias, w_proj_a, w_proj_b, w_out):
    yp = x_prompt
    ys = x_sample
    p_conv, p_delta, p_k, p_v = [], [], [], []
    s_conv, s_delta, s_k, s_v = [], [], [], []
    Bp = x_prompt.shape[0]
    for l in range(DEPTH):
        params = (norm_w[l], ada_w[l], ada_b[l], w_in[l], conv_w[l], gdn_a_log[l], gdn_dt_bias[l],
                  gdn_norm_w[l], q_norm_w[l], k_norm_w[l], rel_bias[l], w_proj_a[l], w_proj_b[l], w_out[l])
        zero_buf = jnp.zeros((Bp, CONV_WIDTH - 1, CONV_DIM), x_prompt.dtype)
        zero_state = jnp.zeros((Bp, GDN_HEADS, GDN_DK, GDN_DV), state_delta.dtype)
        yp, cb, sd, kk, vv = layer(yp, c_prompt, zero_buf, zero_state, None, None, params, True)
        p_conv.append(cb); p_delta.append(sd); p_k.append(kk); p_v.append(vv)
        ys, cb, sd, kk, vv = layer(ys, c_sample, state_conv[l], state_delta[l], cache_k[l], cache_v[l],
                                   params, False)
        s_conv.append(cb); s_delta.append(sd); s_k.append(kk); s_v.append(vv)
    return (yp, ys,
            jnp.stack(p_conv), jnp.stack(p_delta), jnp.stack(p_k), jnp.stack(p_v),
            jnp.stack(s_conv), jnp.stack(s_delta), jnp.stack(s_k), jnp.stack(s_v))
```

```python
import functools
import math

import numpy as np
import jax
import jax.numpy as jnp
from jax import lax
from jax.experimental import pallas as pl
from jax.experimental.pallas import tpu as pltpu

F32 = jnp.float32
BF16 = jnp.bfloat16
HIGHEST = lax.Precision.HIGHEST

EPS = 1e-6
CHUNK = 64
BAND_PREV = 8
REL_CLIP = 128
N_REL = 2 * REL_CLIP + 1
PAST_LEN = 2048
CONV_WIDTH = 4
HEAD_DIM = 128
LANES = 128
SUBLANES = 8
VMEM_LIMIT_BYTES = 56 * 1024 * 1024
GDN_BLOCK = 64
GDN_SUPER = 256
PREV_ROWS = BAND_PREV * CHUNK
BAND = PREV_ROWS + CHUNK
BIAS_COLS = 640


def _dot(a, b):
    return jnp.dot(a, b, preferred_element_type=F32)


def _dot_nt(a, b):
    return lax.dot_general(a, b, (((1,), (1,)), ((), ())), preferred_element_type=F32)


def _dot_tn(a, b):
    return lax.dot_general(a, b, (((0,), (0,)), ((), ())), preferred_element_type=F32)


def _silu(x):
    return x * jax.nn.sigmoid(x)


def _rms(x, w):
    return x * lax.rsqrt(jnp.mean(x * x, axis=-1, keepdims=True) + EPS) * w


def _params(*sem):
    return pltpu.CompilerParams(dimension_semantics=sem, vmem_limit_bytes=VMEM_LIMIT_BYTES)


def _largest_divisor(n, candidates):
    for c in candidates:
        if n % c == 0:
            return c
    raise ValueError(f"no tile in {candidates} divides {n}")


def _row_tiling(batch, seq, rows):
    if seq >= rows:
        assert seq % rows == 0
        return 1, rows
    bb = min(batch, rows // seq)
    assert batch % bb == 0
    return bb, seq


def _mod_kernel(c_ref, w_ref, b_ref, o_ref):
    o_ref[0] = jnp.dot(_silu(c_ref[...]), w_ref[0], precision=HIGHEST,
                       preferred_element_type=F32) + b_ref[0]


def _mod_call(c, ada_w, ada_b):
    depth, d, n = ada_w.shape
    rows = c.shape[0]
    tn = _largest_divisor(n, (1024, 512, 256, 128))
    return pl.pallas_call(
        _mod_kernel,
        grid=(depth, n // tn),
        in_specs=[pl.BlockSpec((rows, d), lambda l, j: (0, 0)),
                  pl.BlockSpec((1, d, tn), lambda l, j: (l, 0, j)),
                  pl.BlockSpec((1, 1, tn), lambda l, j: (l, 0, j))],
        out_specs=pl.BlockSpec((1, rows, tn), lambda l, j: (l, 0, j)),
        out_shape=jax.ShapeDtypeStruct((depth, rows, n), F32),
        compiler_params=_params("parallel", "parallel"),
        name="adaln_mod",
    )(c, ada_w, ada_b.reshape(depth, 1, n))


def _bias_kernel(rb_ref, o_ref):
    rb = rb_ref[0]
    nrel = rb.shape[1]
    r = lax.broadcasted_iota(jnp.int32, (nrel, BIAS_COLS), 0)
    j = lax.broadcasted_iota(jnp.int32, (nrel, BIAS_COLS), 1)

    def body(i, carry):
        dist = jnp.clip(PREV_ROWS + i - j, -REL_CLIP, REL_CLIP) + REL_CLIP
        onehot = (dist == r).astype(F32)
        o_ref[0, i] = jnp.dot(rb, onehot, precision=HIGHEST, preferred_element_type=F32)
        return carry

    lax.fori_loop(0, CHUNK, body, 0)


def _bias_call(rel_bias):
    depth, heads, nrel = rel_bias.shape
    nrel_pad = -(-nrel // SUBLANES) * SUBLANES
    rb = jnp.pad(rel_bias, ((0, 0), (0, 0), (0, nrel_pad - nrel)))
    tab = pl.pallas_call(
        _bias_kernel,
        grid=(depth,),
        in_specs=[pl.BlockSpec((1, heads, nrel_pad), lambda l: (l, 0, 0))],
        out_specs=pl.BlockSpec((1, CHUNK, heads, BIAS_COLS), lambda l: (l, 0, 0, 0)),
        out_shape=jax.ShapeDtypeStruct((depth, CHUNK, heads, BIAS_COLS), F32),
        compiler_params=_params("parallel"),
        name="rel_bias_table",
    )(rb)
    return jnp.transpose(tab, (0, 2, 1, 3))


def _inproj_kernel(x_ref, sc_ref, sh_ref, nw_ref, w_ref, ws_ref, z_ref, zs_ref, h_ref, *, bb, tt, rc):
    @pl.when(pl.program_id(2) == 0)
    def _():
        for b in range(bb):
            for r0 in range(0, tt, rc):
                x = x_ref[b, r0:r0 + rc, :]
                h = _rms(x, nw_ref[...]) * (1.0 + sc_ref[b]) + sh_ref[b]
                h_ref[b * tt + r0:b * tt + r0 + rc, :] = h.astype(BF16)
        zs_ref[...] = _dot(h_ref[...], ws_ref[...]).reshape(zs_ref.shape)

    z_ref[...] = _dot(h_ref[...], w_ref[...]).reshape(z_ref.shape)


def _inproj_call(x, scale, shift, norm_w, w_main, w_small):
    batch, seq, d = x.shape
    n = w_main.shape[1]
    ns = w_small.shape[1]
    bb, tt = _row_tiling(batch, seq, 1024)
    tn = _largest_divisor(n, (1024, 512, 256, 128))
    kern = functools.partial(_inproj_kernel, bb=bb, tt=tt, rc=min(tt, 256))
    return pl.pallas_call(
        kern,
        grid=(batch // bb, seq // tt, n // tn),
        in_specs=[pl.BlockSpec((bb, tt, d), lambda b, i, j: (b, i, 0)),
                  pl.BlockSpec((bb, 1, d), lambda b, i, j: (b, 0, 0)),
                  pl.BlockSpec((bb, 1, d), lambda b, i, j: (b, 0, 0)),
                  pl.BlockSpec((1, d), lambda b, i, j: (0, 0)),
                  pl.BlockSpec((d, tn), lambda b, i, j: (0, j)),
                  pl.BlockSpec((d, ns), lambda b, i, j: (0, 0))],
        out_specs=[pl.BlockSpec((bb, tt, tn), lambda b, i, j: (b, i, j)),
                   pl.BlockSpec((bb, tt, ns), lambda b, i, j: (b, i, 0))],
        out_shape=[jax.ShapeDtypeStruct((batch, seq, n), F32),
                   jax.ShapeDtypeStruct((batch, seq, ns), F32)],
        scratch_shapes=[pltpu.VMEM((bb * tt, d), BF16)],
        compiler_params=_params("parallel", "parallel", "arbitrary"),
        name="in_proj",
    )(x, scale, shift, norm_w.reshape(1, d), w_main, w_small)


def _gdn_kernel(alog_ref, dt_ref, q_ref, k_ref, v_ref, gate_ref, r_ref,
                cwq_ref, cwk_ref, cwv_ref, csq_ref, csk_ref, csv_ref, s0_ref, nw_ref,
                o_ref, sfin_ref, xs_ref, s_ref, *, tb, sc, blk):
    head = pl.program_id(1)
    t = pl.program_id(2)
    pad = SUBLANES
    taps = CONV_WIDTH

    @pl.when(t == 0)
    def _():
        xs_ref[0, 0:pad, :] = csq_ref[0]
        xs_ref[1, 0:pad, :] = csk_ref[0]
        xs_ref[2, 0:pad, :] = csv_ref[0]
        s_ref[...] = s0_ref[0, 0]

    xs_ref[0, pad:pad + tb, :] = q_ref[0]
    xs_ref[1, pad:pad + tb, :] = k_ref[0]
    xs_ref[2, pad:pad + tb, :] = v_ref[0]

    def conv_act(i, cw_ref):
        w = cw_ref[...]
        base = pad - (taps - 1)
        acc = xs_ref[i, base:base + tb, :] * w[0:1]
        for j in range(1, taps):
            acc = acc + xs_ref[i, base + j:base + j + tb, :] * w[j:j + 1]
        return _silu(acc)

    aq = conv_act(0, cwq_ref)
    ak = conv_act(1, cwk_ref)
    v = conv_act(2, cwv_ref)
    for i in range(3):
        xs_ref[i, 0:pad, :] = xs_ref[i, tb:tb + pad, :]

    q = aq * lax.rsqrt(jnp.sum(aq * aq, axis=-1, keepdims=True) + EPS) * (HEAD_DIM ** -0.5)
    k = ak * lax.rsqrt(jnp.sum(ak * ak, axis=-1, keepdims=True) + EPS)

    r = r_ref[0, 0]
    beta_rows = jax.nn.sigmoid(r)
    xx = r + jnp.full(r.shape, dt_ref[head], F32)
    softplus = jnp.maximum(xx, 0.0) + jnp.log1p(jnp.exp(-jnp.abs(xx)))
    g_rows = -jnp.exp(jnp.full(r.shape, alog_ref[head], F32)) * softplus

    shift = int(math.log2(blk))
    ri = lax.broadcasted_iota(jnp.int32, (sc, sc), 0)
    ci = lax.broadcasted_iota(jnp.int32, (sc, sc), 1)
    same = (ri >> shift) == (ci >> shift)
    incl = same & (ri >= ci)
    strict = same & (ri > ci)
    sum_mats = jnp.concatenate([(same & (ri <= ci)).astype(F32), same.astype(F32)], axis=1)
    eye = (ri == ci).astype(F32)
    row_id = lax.broadcasted_iota(jnp.int32, (SUBLANES, sc), 0)

    def split16(x):
        hi = x.astype(BF16)
        return hi, (x - hi.astype(F32)).astype(BF16)

    for c in range(tb // sc):
        sl = slice(c * sc, (c + 1) * sc)
        sums = jnp.dot(g_rows[:, sl], sum_mats, precision=HIGHEST, preferred_element_type=F32)
        g_cum_rows = sums[:, :sc]
        cols = jnp.transpose(jnp.where(row_id == 0, beta_rows[:, sl], g_cum_rows))
        beta_c = cols[:, 0:1]
        g_c = cols[:, 1:2]
        g_last_c = jnp.transpose(sums[:, sc:])[:, 1:2]
        g_r = g_cum_rows[1:2, :]
        decay = jnp.where(incl, jnp.exp(jnp.where(incl, g_c - g_r, 0.0)), 0.0)

        kc = k[sl]
        qc = q[sl]
        k16 = kc.astype(BF16)
        kk = _dot_nt(k16, k16)
        qk = _dot_nt(qc.astype(BF16), k16)
        neg_m = jnp.where(strict, -(beta_c * kk) * decay, 0.0)
        a_intra = qk * decay

        n_hi, n_lo = split16(neg_m)
        x = eye + neg_m
        steps = shift - 1
        for step in range(steps):
            if step < steps - 1:
                nx = _dot(n_hi, x.astype(BF16))
            else:
                x_hi, x_lo = split16(x)
                nx = _dot(n_hi, x_hi) + (_dot(n_hi, x_lo) + _dot(n_lo, x_hi))
            resid = (eye - x) + nx
            x = x + _dot(x.astype(BF16), resid.astype(BF16))
        t_inv = x

        e_g = jnp.exp(g_c)
        rhs = jnp.concatenate([v[sl] * beta_c, kc * (beta_c * e_g)], axis=1).astype(BF16)
        uw = _dot(t_inv.astype(BF16), rhs)
        u = uw[:, :HEAD_DIM]
        w = uw[:, HEAD_DIM:]
        qg = qc * e_g
        kg = kc * jnp.exp(g_last_c - g_c)
        e_last = jnp.exp(g_last_c)

        s = s_ref[...]
        v_new, o_inter = [], []
        for b in range(sc // blk):
            rs = slice(b * blk, (b + 1) * blk)
            ws_qs = _dot(jnp.concatenate([w[rs], qg[rs]], axis=0).astype(BF16), s.astype(BF16))
            v_new.append(u[rs] - ws_qs[:blk])
            o_inter.append(ws_qs[blk:])
            s = s * e_last[b * blk:b * blk + 1, :] + _dot_tn(kg[rs].astype(BF16), v_new[-1].astype(BF16))
        s_ref[...] = s
        v_new = jnp.concatenate(v_new, axis=0)
        o = jnp.concatenate(o_inter, axis=0) + _dot(a_intra.astype(BF16), v_new.astype(BF16))

        o_ref[0, sl, :] = (_rms(o, nw_ref[...]) * _silu(gate_ref[0, sl, :])).astype(BF16)

    @pl.when(t == pl.num_programs(2) - 1)
    def _():
        sfin_ref[0, 0] = s_ref[...]


def _gdn_call(z, rows, conv_w8, conv_buf8, s0, a_log, dt_bias, norm_w, heads):
    batch, seq, _ = z.shape
    tb = min(seq, 512)
    sc = min(tb, GDN_SUPER)
    blk = min(sc, GDN_BLOCK)
    assert seq % tb == 0 and tb % sc == 0 and sc % blk == 0
    hd = HEAD_DIM
    smem = pl.BlockSpec(memory_space=pltpu.SMEM)

    def zcol(off):
        return pl.BlockSpec((1, tb, hd), lambda b, h, t: (b, t, off + h))

    def cw(off):
        return pl.BlockSpec((SUBLANES, hd), lambda b, h, t: (0, off + h))

    def cs(off):
        return pl.BlockSpec((1, SUBLANES, hd), lambda b, h, t: (b, 0, off + h))

    kern = functools.partial(_gdn_kernel, tb=tb, sc=sc, blk=blk)
    return pl.pallas_call(
        kern,
        grid=(batch, heads, seq // tb),
        in_specs=[smem, smem,
                  zcol(0), zcol(heads), zcol(2 * heads), zcol(3 * heads),
                  pl.BlockSpec((1, 1, SUBLANES, tb), lambda b, h, t: (b, h, 0, t)),
                  cw(0), cw(heads), cw(2 * heads),
                  cs(0), cs(heads), cs(2 * heads),
                  pl.BlockSpec((1, 1, hd, hd), lambda b, h, t: (b, h, 0, 0)),
                  pl.BlockSpec((1, hd), lambda b, h, t: (0, 0))],
        out_specs=[pl.BlockSpec((1, tb, hd), lambda b, h, t: (b, t, h)),
                   pl.BlockSpec((1, 1, hd, hd), lambda b, h, t: (b, h, 0, 0))],
        out_shape=[jax.ShapeDtypeStruct((batch, seq, heads * hd), BF16),
                   jax.ShapeDtypeStruct((batch, heads, hd, hd), F32)],
        scratch_shapes=[pltpu.VMEM((3, tb + SUBLANES, hd), F32),
                        pltpu.VMEM((hd, hd), F32)],
        compiler_params=_params("parallel", "parallel", "arbitrary"),
        name="gated_deltanet",
    )(a_log, dt_bias, z, z, z, z, rows, conv_w8, conv_w8, conv_w8,
      conv_buf8, conv_buf8, conv_buf8, s0, norm_w.reshape(1, hd))


def _attn_prompt_kernel(q_ref, k_ref, v_ref, gate_ref, bias_ref, qw_ref, kw_ref,
                        o_ref, kc_ref, kbuf, vbuf, qs, *, tq):
    i = pl.program_id(2)

    @pl.when(i == 0)
    def _():
        kbuf[0:PREV_ROWS, :] = jnp.zeros((PREV_ROWS, HEAD_DIM), BF16)
        vbuf[0:PREV_ROWS, :] = jnp.zeros((PREV_ROWS, HEAD_DIM), BF16)

    @pl.when(i > 0)
    def _():
        kbuf[0:PREV_ROWS, :] = kbuf[tq:tq + PREV_ROWS, :]
        vbuf[0:PREV_ROWS, :] = vbuf[tq:tq + PREV_ROWS, :]

    kn = _rms(k_ref[0], kw_ref[...])
    kc_ref[0] = kn
    kbuf[PREV_ROWS:PREV_ROWS + tq, :] = kn.astype(BF16)
    vbuf[PREV_ROWS:PREV_ROWS + tq, :] = v_ref[0].astype(BF16)
    qs[...] = (_rms(q_ref[0], qw_ref[...]) * (HEAD_DIM ** -0.5)).astype(BF16)

    bias = bias_ref[0][:, :BAND]
    col = lax.broadcasted_iota(jnp.int32, (CHUNK, BAND), 1)
    for c in range(tq // CHUNK):
        rows = slice(c * CHUNK, (c + 1) * CHUNK)
        band = slice(c * CHUNK, c * CHUNK + BAND)
        s = _dot_nt(qs[rows, :], kbuf[band, :]) + bias
        n_before_start = jnp.where(i == 0, PREV_ROWS - c * CHUNK, 0)
        s = jnp.where(col < n_before_start, -1e30, s)
        p = jnp.exp(s - jnp.max(s, axis=-1, keepdims=True))
        o = _dot(p.astype(BF16), vbuf[band, :]) / jnp.sum(p, axis=-1, keepdims=True)
        o_ref[0, rows, :] = (o * _silu(gate_ref[0, rows, :])).astype(BF16)


def _attn_prompt_call(z, bias_tab, q_norm_w, k_norm_w, heads):
    batch, seq, _ = z.shape
    tq = PREV_ROWS
    assert seq % tq == 0
    hd = HEAD_DIM

    def zcol(off):
        return pl.BlockSpec((1, tq, hd), lambda b, h, i: (b, i, off + h))

    kern = functools.partial(_attn_prompt_kernel, tq=tq)
    return pl.pallas_call(
        kern,
        grid=(batch, heads, seq // tq),
        in_specs=[zcol(4 * heads), zcol(5 * heads), zcol(6 * heads), zcol(7 * heads),
                  pl.BlockSpec((1, CHUNK, BIAS_COLS), lambda b, h, i: (h, 0, 0)),
                  pl.BlockSpec((1, hd), lambda b, h, i: (0, 0)),
                  pl.BlockSpec((1, hd), lambda b, h, i: (0, 0))],
        out_specs=[pl.BlockSpec((1, tq, hd), lambda b, h, i: (b, i, h)),
                   pl.BlockSpec((1, tq, hd), lambda b, h, i: (b, 0, h))],
        out_shape=[jax.ShapeDtypeStruct((batch, seq, heads * hd), BF16),
                   jax.ShapeDtypeStruct((batch, tq, heads * hd), F32)],
        scratch_shapes=[pltpu.VMEM((PREV_ROWS + tq, hd), BF16),
                        pltpu.VMEM((PREV_ROWS + tq, hd), BF16),
                        pltpu.VMEM((tq, hd), BF16)],
        compiler_params=_params("parallel", "parallel", "arbitrary"),
        name="band_attention_prompt",
    )(z, z, z, z, bias_tab, q_norm_w.reshape(1, hd), k_norm_w.reshape(1, hd))


def _attn_sample_kernel(q_ref, k_ref, v_ref, gate_ref, ck_ref, cv_ref, bias_ref, qw_ref, kw_ref,
                        o_ref, kn_ref, *, seq, n_cached):
    kn = _rms(k_ref[0], kw_ref[...])
    kn_ref[0] = kn
    q16 = (_rms(q_ref[0], qw_ref[...]) * (HEAD_DIM ** -0.5)).astype(BF16)
    bias = bias_ref[0]
    s_old = _dot_nt(q16, ck_ref[0].astype(BF16)) + bias[:seq, :n_cached]
    s_new = _dot_nt(q16, kn.astype(BF16)) + bias[:seq, n_cached:n_cached + seq]
    m = jnp.maximum(jnp.max(s_old, axis=-1, keepdims=True), jnp.max(s_new, axis=-1, keepdims=True))
    p_old = jnp.exp(s_old - m)
    p_new = jnp.exp(s_new - m)
    denom = jnp.sum(p_old, axis=-1, keepdims=True) + jnp.sum(p_new, axis=-1, keepdims=True)
    o = (_dot(p_old.astype(BF16), cv_ref[0].astype(BF16))
         + _dot(p_new.astype(BF16), v_ref[0].astype(BF16))) / denom
    o_ref[0] = (o * _silu(gate_ref[0])).astype(BF16)


def _attn_sample_call(z, cache_k, cache_v, bias_tab, q_norm_w, k_norm_w, heads):
    batch, seq, _ = z.shape
    n_cached = cache_k.shape[1]
    hd = HEAD_DIM
    q_pos = PAST_LEN + np.arange(seq)
    k_pos = PAST_LEN - n_cached + np.arange(n_cached + seq)
    qc, kc = q_pos // CHUNK, k_pos // CHUNK
    valid = (k_pos[None] >= 0) & (kc[None] <= qc[:, None]) & (kc[None] >= qc[:, None] - BAND_PREV)
    assert valid.all() and n_cached == PREV_ROWS and seq <= CHUNK and n_cached + seq <= BIAS_COLS

    def zcol(off):
        return pl.BlockSpec((1, seq, hd), lambda b, h: (b, 0, off + h))

    cache_spec = pl.BlockSpec((1, n_cached, hd), lambda b, h: (b, 0, h))
    kern = functools.partial(_attn_sample_kernel, seq=seq, n_cached=n_cached)
    return pl.pallas_call(
        kern,
        grid=(batch, heads),
        in_specs=[zcol(4 * heads), zcol(5 * heads), zcol(6 * heads), zcol(7 * heads),
                  cache_spec, cache_spec,
                  pl.BlockSpec((1, CHUNK, BIAS_COLS), lambda b, h: (h, 0, 0)),
                  pl.BlockSpec((1, hd), lambda b, h: (0, 0)),
                  pl.BlockSpec((1, hd), lambda b, h: (0, 0))],
        out_specs=[pl.BlockSpec((1, seq, hd), lambda b, h: (b, 0, h)),
                   pl.BlockSpec((1, seq, hd), lambda b, h: (b, 0, h))],
        out_shape=[jax.ShapeDtypeStruct((batch, seq, heads * hd), BF16),
                   jax.ShapeDtypeStruct((batch, seq, heads * hd), F32)],
        compiler_params=_params("parallel", "parallel"),
        name="band_attention_sample",
    )(z, z, z, z, cache_k.reshape(batch, n_cached, heads * hd),
      cache_v.reshape(batch, n_cached, heads * hd), bias_tab,
      q_norm_w.reshape(1, hd), k_norm_w.reshape(1, hd))


def _out_kernel(oa_ref, ob_ref, ma_ref, mb_ref, x_ref, g_ref, wa_ref, wb_ref, wo_ref, y_ref):
    bb, tt, d = x_ref.shape
    ya = _dot(oa_ref[...].reshape(bb * tt, -1), wa_ref[...])
    yb = _dot(ob_ref[...].reshape(bb * tt, -1), wb_ref[...])
    merged = (jax.nn.sigmoid(ma_ref[...].reshape(bb * tt, d)) * ya
              + jax.nn.sigmoid(mb_ref[...].reshape(bb * tt, d)) * yb)
    delta = _dot(merged.astype(BF16), wo_ref[...]).reshape(bb, tt, d)
    y_ref[...] = x_ref[...] + g_ref[...] * delta


def _out_call(o_a, o_b, z, x, gate, w_proj_a, w_proj_b, w_out, heads):
    batch, seq, d = x.shape
    bb, tt = _row_tiling(batch, seq, 256)
    wa = o_a.shape[-1]
    wb = o_b.shape[-1]
    m_off = (8 * heads * HEAD_DIM) // d
    assert m_off * d == 8 * heads * HEAD_DIM

    def rows(width, off=0):
        return pl.BlockSpec((bb, tt, width), lambda b, i: (b, i, off))

    def whole(shape):
        return pl.BlockSpec(shape, lambda b, i: (0, 0), pipeline_mode=pl.Buffered(1))

    return pl.pallas_call(
        _out_kernel,
        grid=(batch // bb, seq // tt),
        in_specs=[rows(wa), rows(wb), rows(d, m_off), rows(d, m_off + 1), rows(d),
                  pl.BlockSpec((bb, 1, d), lambda b, i: (b, 0, 0)),
                  whole((wa, d)), whole((wb, d)), whole((d, d))],
        out_specs=rows(d),
        out_shape=jax.ShapeDtypeStruct((batch, seq, d), F32),
        compiler_params=_params("parallel", "parallel"),
        name="out_proj",
    )(o_a, o_b, z, z, x, gate, w_proj_a, w_proj_b, w_out)


def _layer(x, mod, conv_buf, s0, cache_k, cache_v, p, bias_tab, heads, prompt):
    (norm_w, w_main, w_small, conv_w8, a_log, dt_bias, gdn_norm_w,
     q_norm_w, k_norm_w, w_proj_a, w_proj_b, w_out) = p
    batch, seq, d = x.shape
    hd = HEAD_DIM
    shift, scale, gate = [m.reshape(batch, 1, d) for m in jnp.split(mod, 3, axis=-1)]

    z, z_small = _inproj_call(x, scale, shift, norm_w, w_main, w_small)

    logits = jnp.transpose(z_small[..., :2 * heads].reshape(batch, seq, 2, heads), (0, 3, 2, 1))
    rows = jnp.pad(logits, ((0, 0), (0, 0), (0, SUBLANES - 2), (0, 0)))
    conv_buf8 = jnp.pad(conv_buf, ((0, 0), (SUBLANES - (CONV_WIDTH - 1), 0), (0, 0)))
    o_a, s_new = _gdn_call(z, rows, conv_w8, conv_buf8, s0, a_log, dt_bias, gdn_norm_w, heads)

    conv_dim = 3 * heads * hd
    new_buf = z[:, seq - (CONV_WIDTH - 1):, :conv_dim]
    if prompt:
        o_b, k_new = _attn_prompt_call(z, bias_tab, q_norm_w, k_norm_w, heads)
        keep = k_new.shape[1]
        v_new = z[:, seq - keep:, 6 * heads * hd:7 * heads * hd]
    else:
        o_b, k_new = _attn_sample_call(z, cache_k, cache_v, bias_tab, q_norm_w, k_norm_w, heads)
        keep = seq
        v_new = z[:, :, 6 * heads * hd:7 * heads * hd]
    k_new = k_new.reshape(batch, keep, heads, hd)
    v_new = v_new.reshape(batch, keep, heads, hd)

    y = _out_call(o_a, o_b, z, x, gate, w_proj_a, w_proj_b, w_out, heads)
    return y, new_buf, s_new, k_new, v_new


def kernel(x_prompt, x_sample, c_prompt, c_sample, state_conv, state_delta, cache_k, cache_v, norm_w, ada_w, ada_b, w_in, conv_w, gdn_a_log, gdn_dt_bias, gdn_norm_w, q_norm_w, k_norm_w, rel_bias, w_proj_a, w_proj_b, w_out):
    bp, _, d = x_prompt.shape
    depth = w_in.shape[0]
    heads = gdn_a_log.shape[1]
    hd = HEAD_DIM
    small0 = 4 * heads * hd
    assert w_in.shape[2] == 8 * heads * hd + 2 * heads + 2 * d
    assert conv_w.shape[1:] == (CONV_WIDTH, 3 * heads * hd) and rel_bias.shape[1:] == (heads, N_REL)

    mod = _mod_call(jnp.concatenate([c_prompt, c_sample], axis=0), ada_w, ada_b)
    bias_tab = _bias_call(rel_bias)

    yp, ys = x_prompt, x_sample
    outs_p, outs_s = [], []
    zero_buf = jnp.zeros((bp, CONV_WIDTH - 1, 3 * heads * hd), x_prompt.dtype)
    zero_state = jnp.zeros((bp, heads, hd, hd), state_delta.dtype)
    for l in range(depth):
        w = w_in[l]
        w_main = jnp.concatenate([w[:, :small0], w[:, small0 + 2 * heads:]], axis=1).astype(BF16)
        w_small = jnp.pad(w[:, small0:small0 + 2 * heads], ((0, 0), (0, LANES - 2 * heads))).astype(BF16)
        conv_w8 = jnp.pad(conv_w[l], ((0, SUBLANES - CONV_WIDTH), (0, 0)))
        p = (norm_w[l], w_main, w_small, conv_w8, gdn_a_log[l], gdn_dt_bias[l], gdn_norm_w[l],
             q_norm_w[l], k_norm_w[l], w_proj_a[l].astype(BF16), w_proj_b[l].astype(BF16),
             w_out[l].astype(BF16))
        yp, *rest = _layer(yp, mod[l, :bp], zero_buf, zero_state, None, None, p, bias_tab[l], heads, True)
        outs_p.append(rest)
        ys, *rest = _layer(ys, mod[l, bp:], state_conv[l], state_delta[l], cache_k[l], cache_v[l],
                           p, bias_tab[l], heads, False)
        outs_s.append(rest)

    stack = lambda outs, i: jnp.stack([o[i] for o in outs])
    return (yp, ys,
            stack(outs_p, 0), stack(outs_p, 1), stack(outs_p, 2), stack(outs_p, 3),
            stack(outs_s, 0), stack(outs_s, 1), stack(outs_s, 2), stack(outs_s, 3))
```

```python
import functools
import math

import numpy as np
import jax
import jax.numpy as jnp
from jax import lax
from jax.experimental import pallas as pl
from jax.experimental.pallas import tpu as pltpu

F32 = jnp.float32
BF16 = jnp.bfloat16
HIGHEST = lax.Precision.HIGHEST

EPS = 1e-6
CHUNK = 64
BAND_PREV = 8
REL_CLIP = 128
N_REL = 2 * REL_CLIP + 1
PAST_LEN = 2048
CONV_WIDTH = 4
HEAD_DIM = 128
LANES = 128
SUBLANES = 8
VMEM_LIMIT_BYTES = 56 * 1024 * 1024
GDN_BLOCK = 64
GDN_SUPER = 256
GDN_HEADS_PER_STEP = 4
PREV_ROWS = BAND_PREV * CHUNK
BAND = PREV_ROWS + CHUNK
BIAS_COLS = 640


def _dot(a, b):
    return jnp.dot(a, b, preferred_element_type=F32)


def _dot_nt(a, b):
    return lax.dot_general(a, b, (((1,), (1,)), ((), ())), preferred_element_type=F32)


def _dot_tn(a, b):
    return lax.dot_general(a, b, (((0,), (0,)), ((), ())), preferred_element_type=F32)


def _silu(x):
    return x * jax.nn.sigmoid(x)


def _rms(x, w):
    return x * lax.rsqrt(jnp.mean(x * x, axis=-1, keepdims=True) + EPS) * w


def _params(*sem):
    return pltpu.CompilerParams(dimension_semantics=sem, vmem_limit_bytes=VMEM_LIMIT_BYTES)


def _largest_divisor(n, candidates):
    for c in candidates:
        if n % c == 0:
            return c
    raise ValueError(f"no tile in {candidates} divides {n}")


def _row_tiling(batch, seq, rows):
    if seq >= rows:
        assert seq % rows == 0
        return 1, rows
    bb = min(batch, rows // seq)
    assert batch % bb == 0
    return bb, seq


def _mod_kernel(c_ref, w_ref, b_ref, o_ref):
    o_ref[0] = jnp.dot(_silu(c_ref[...]), w_ref[0], precision=HIGHEST,
                       preferred_element_type=F32) + b_ref[0]


def _mod_call(c, ada_w, ada_b):
    depth, d, n = ada_w.shape
    rows = c.shape[0]
    tn = _largest_divisor(n, (1024, 512, 256, 128))
    return pl.pallas_call(
        _mod_kernel,
        grid=(depth, n // tn),
        in_specs=[pl.BlockSpec((rows, d), lambda l, j: (0, 0)),
                  pl.BlockSpec((1, d, tn), lambda l, j: (l, 0, j)),
                  pl.BlockSpec((1, 1, tn), lambda l, j: (l, 0, j))],
        out_specs=pl.BlockSpec((1, rows, tn), lambda l, j: (l, 0, j)),
        out_shape=jax.ShapeDtypeStruct((depth, rows, n), F32),
        compiler_params=_params("parallel", "parallel"),
        name="adaln_mod",
    )(c, ada_w, ada_b.reshape(depth, 1, n))


def _bias_kernel(rb_ref, o_ref):
    rb = rb_ref[0]
    nrel = rb.shape[1]
    r = lax.broadcasted_iota(jnp.int32, (nrel, BIAS_COLS), 0)
    j = lax.broadcasted_iota(jnp.int32, (nrel, BIAS_COLS), 1)

    def body(i, carry):
        dist = jnp.clip(PREV_ROWS + i - j, -REL_CLIP, REL_CLIP) + REL_CLIP
        onehot = (dist == r).astype(F32)
        o_ref[0, i] = jnp.dot(rb, onehot, precision=HIGHEST, preferred_element_type=F32)
        return carry

    lax.fori_loop(0, CHUNK, body, 0)


def _bias_call(rel_bias):
    depth, heads, nrel = rel_bias.shape
    nrel_pad = -(-nrel // SUBLANES) * SUBLANES
    rb = jnp.pad(rel_bias, ((0, 0), (0, 0), (0, nrel_pad - nrel)))
    tab = pl.pallas_call(
        _bias_kernel,
        grid=(depth,),
        in_specs=[pl.BlockSpec((1, heads, nrel_pad), lambda l: (l, 0, 0))],
        out_specs=pl.BlockSpec((1, CHUNK, heads, BIAS_COLS), lambda l: (l, 0, 0, 0)),
        out_shape=jax.ShapeDtypeStruct((depth, CHUNK, heads, BIAS_COLS), F32),
        compiler_params=_params("parallel"),
        name="rel_bias_table",
    )(rb)
    return jnp.transpose(tab, (0, 2, 1, 3))


def _inproj_kernel(x_ref, sc_ref, sh_ref, nw_ref, w_ref, ws_ref, z_ref, zs_ref, h_ref, *, bb, tt, rc):
    @pl.when(pl.program_id(2) == 0)
    def _():
        for b in range(bb):
            for r0 in range(0, tt, rc):
                x = x_ref[b, r0:r0 + rc, :]
                h = _rms(x, nw_ref[...]) * (1.0 + sc_ref[b]) + sh_ref[b]
                h_ref[b * tt + r0:b * tt + r0 + rc, :] = h.astype(BF16)
        zs_ref[...] = _dot(h_ref[...], ws_ref[...]).reshape(zs_ref.shape)

    z_ref[...] = _dot(h_ref[...], w_ref[...]).reshape(z_ref.shape)


def _inproj_call(x, scale, shift, norm_w, w_main, w_small):
    batch, seq, d = x.shape
    n = w_main.shape[1]
    ns = w_small.shape[1]
    bb, tt = _row_tiling(batch, seq, 1024)
    tn = _largest_divisor(n, (1024, 512, 256, 128))
    kern = functools.partial(_inproj_kernel, bb=bb, tt=tt, rc=min(tt, 256))
    return pl.pallas_call(
        kern,
        grid=(batch // bb, seq // tt, n // tn),
        in_specs=[pl.BlockSpec((bb, tt, d), lambda b, i, j: (b, i, 0)),
                  pl.BlockSpec((bb, 1, d), lambda b, i, j: (b, 0, 0)),
                  pl.BlockSpec((bb, 1, d), lambda b, i, j: (b, 0, 0)),
                  pl.BlockSpec((1, d), lambda b, i, j: (0, 0)),
                  pl.BlockSpec((d, tn), lambda b, i, j: (0, j)),
                  pl.BlockSpec((d, ns), lambda b, i, j: (0, 0))],
        out_specs=[pl.BlockSpec((bb, tt, tn), lambda b, i, j: (b, i, j)),
                   pl.BlockSpec((bb, tt, ns), lambda b, i, j: (b, i, 0))],
        out_shape=[jax.ShapeDtypeStruct((batch, seq, n), F32),
                   jax.ShapeDtypeStruct((batch, seq, ns), F32)],
        scratch_shapes=[pltpu.VMEM((bb * tt, d), BF16)],
        compiler_params=_params("parallel", "parallel", "arbitrary"),
        name="in_proj",
    )(x, scale, shift, norm_w.reshape(1, d), w_main, w_small)


def _gdn_kernel(alog_ref, dt_ref, q_ref, k_ref, v_ref, gate_ref, r_ref,
                cwq_ref, cwk_ref, cwv_ref, csq_ref, csk_ref, csv_ref, s0_ref, nw_ref,
                o_ref, sfin_ref, xs_ref, s_ref, *, tb, blk, hg):
    head0 = pl.program_id(1) * hg
    t = pl.program_id(2)
    pad = SUBLANES
    taps = CONV_WIDTH
    hd = HEAD_DIM
    heads = range(hg)

    @pl.when(t == 0)
    def _():
        xs_ref[0, 0:pad, :] = csq_ref[0]
        xs_ref[1, 0:pad, :] = csk_ref[0]
        xs_ref[2, 0:pad, :] = csv_ref[0]
        s_ref[...] = s0_ref[0]

    xs_ref[0, pad:pad + tb, :] = q_ref[0]
    xs_ref[1, pad:pad + tb, :] = k_ref[0]
    xs_ref[2, pad:pad + tb, :] = v_ref[0]

    def conv_act(i, cw_ref, h):
        cols = slice(h * hd, (h + 1) * hd)
        w = cw_ref[:, cols]
        base = pad - (taps - 1)
        acc = xs_ref[i, base:base + tb, cols] * w[0:1]
        for j in range(1, taps):
            acc = acc + xs_ref[i, base + j:base + j + tb, cols] * w[j:j + 1]
        return _silu(acc)

    shift = int(math.log2(blk))
    ri = lax.broadcasted_iota(jnp.int32, (tb, tb), 0)
    ci = lax.broadcasted_iota(jnp.int32, (tb, tb), 1)
    same = (ri >> shift) == (ci >> shift)
    incl = same & (ri >= ci)
    strict = same & (ri > ci)
    sum_mats = jnp.concatenate([(same & (ri <= ci)).astype(F32), same.astype(F32)], axis=1)
    eye = (ri == ci).astype(F32)
    row_id = lax.broadcasted_iota(jnp.int32, (SUBLANES, tb), 0)

    def split16(x):
        hi = x.astype(BF16)
        return hi, (x - hi.astype(F32)).astype(BF16)

    q, k, v, neg_m, a_intra, beta_c, g_c, g_last_c = [], [], [], [], [], [], [], []
    for h in heads:
        aq = conv_act(0, cwq_ref, h)
        ak = conv_act(1, cwk_ref, h)
        v.append(conv_act(2, cwv_ref, h))
        q.append(aq * lax.rsqrt(jnp.sum(aq * aq, axis=-1, keepdims=True) + EPS) * (hd ** -0.5))
        k.append(ak * lax.rsqrt(jnp.sum(ak * ak, axis=-1, keepdims=True) + EPS))

        r = r_ref[0, h]
        beta_rows = jax.nn.sigmoid(r)
        xx = r + jnp.full(r.shape, dt_ref[head0 + h], F32)
        softplus = jnp.maximum(xx, 0.0) + jnp.log1p(jnp.exp(-jnp.abs(xx)))
        g_rows = -jnp.exp(jnp.full(r.shape, alog_ref[head0 + h], F32)) * softplus
        sums = jnp.dot(g_rows, sum_mats, precision=HIGHEST, preferred_element_type=F32)
        g_cum_rows = sums[:, :tb]
        cols = jnp.transpose(jnp.where(row_id == 0, beta_rows, g_cum_rows))
        beta_c.append(cols[:, 0:1])
        g_c.append(cols[:, 1:2])
        g_last_c.append(jnp.transpose(sums[:, tb:])[:, 1:2])
        decay = jnp.where(incl, jnp.exp(jnp.where(incl, g_c[h] - g_cum_rows[1:2, :], 0.0)), 0.0)

        k16 = k[h].astype(BF16)
        kk = _dot_nt(k16, k16)
        qk = _dot_nt(q[h].astype(BF16), k16)
        neg_m.append(jnp.where(strict, -(beta_c[h] * kk) * decay, 0.0))
        a_intra.append((qk * decay).astype(BF16))

    for i in range(3):
        xs_ref[i, 0:pad, :] = xs_ref[i, tb:tb + pad, :]

    n_split = [split16(m) for m in neg_m]
    x = [eye + m for m in neg_m]
    steps = shift - 1
    for step in range(steps):
        if step < steps - 1:
            nx = [_dot(n_split[h][0], x[h].astype(BF16)) for h in heads]
        else:
            xs16 = [split16(x[h]) for h in heads]
            nx = [_dot(n_split[h][0], xs16[h][0])
                  + (_dot(n_split[h][0], xs16[h][1]) + _dot(n_split[h][1], xs16[h][0])) for h in heads]
        resid = [((eye - x[h]) + nx[h]).astype(BF16) for h in heads]
        x = [x[h] + _dot(x[h].astype(BF16), resid[h]) for h in heads]

    u, w, qg, kg, e_last = [], [], [], [], []
    for h in heads:
        e_g = jnp.exp(g_c[h])
        rhs = jnp.concatenate([v[h] * beta_c[h], k[h] * (beta_c[h] * e_g)], axis=1).astype(BF16)
        uw = _dot(x[h].astype(BF16), rhs)
        u.append(uw[:, :hd])
        w.append(uw[:, hd:])
        qg.append(q[h] * e_g)
        kg.append((k[h] * jnp.exp(g_last_c[h] - g_c[h])).astype(BF16))
        e_last.append(jnp.exp(g_last_c[h]))

    s = [s_ref[h] for h in heads]
    v_new = [[] for _ in heads]
    o_inter = [[] for _ in heads]
    for b in range(tb // blk):
        rs = slice(b * blk, (b + 1) * blk)
        for h in heads:
            ws_qs = _dot(jnp.concatenate([w[h][rs], qg[h][rs]], axis=0).astype(BF16), s[h].astype(BF16))
            vn = u[h][rs] - ws_qs[:blk]
            v_new[h].append(vn)
            o_inter[h].append(ws_qs[blk:])
            s[h] = s[h] * e_last[h][b * blk:b * blk + 1, :] + _dot_tn(kg[h][rs], vn.astype(BF16))

    for h in heads:
        s_ref[h] = s[h]
        cols = slice(h * hd, (h + 1) * hd)
        vn = jnp.concatenate(v_new[h], axis=0).astype(BF16)
        o = jnp.concatenate(o_inter[h], axis=0) + _dot(a_intra[h], vn)
        o_ref[0, :, cols] = (_rms(o, nw_ref[...]) * _silu(gate_ref[0, :, cols])).astype(BF16)

    @pl.when(t == pl.num_programs(2) - 1)
    def _():
        sfin_ref[0] = s_ref[...]


def _gdn_call(z, rows, conv_w8, conv_buf8, s0, a_log, dt_bias, norm_w, heads):
    batch, seq, _ = z.shape
    tb = min(seq, GDN_SUPER)
    blk = min(tb, GDN_BLOCK)
    hg = min(heads, GDN_HEADS_PER_STEP)
    assert seq % tb == 0 and tb % blk == 0 and heads % hg == 0
    hd = HEAD_DIM
    wd = hg * hd
    ng = heads // hg
    smem = pl.BlockSpec(memory_space=pltpu.SMEM)

    def zcol(seg):
        return pl.BlockSpec((1, tb, wd), lambda b, g, t: (b, t, seg * ng + g))

    def cw(seg):
        return pl.BlockSpec((SUBLANES, wd), lambda b, g, t: (0, seg * ng + g))

    def cs(seg):
        return pl.BlockSpec((1, SUBLANES, wd), lambda b, g, t: (b, 0, seg * ng + g))

    state_spec = pl.BlockSpec((1, hg, hd, hd), lambda b, g, t: (b, g, 0, 0))
    kern = functools.partial(_gdn_kernel, tb=tb, blk=blk, hg=hg)
    return pl.pallas_call(
        kern,
        grid=(batch, ng, seq // tb),
        in_specs=[smem, smem,
                  zcol(0), zcol(1), zcol(2), zcol(3),
                  pl.BlockSpec((1, hg, SUBLANES, tb), lambda b, g, t: (b, g, 0, t)),
                  cw(0), cw(1), cw(2),
                  cs(0), cs(1), cs(2),
                  state_spec,
                  pl.BlockSpec((1, hd), lambda b, g, t: (0, 0))],
        out_specs=[pl.BlockSpec((1, tb, wd), lambda b, g, t: (b, t, g)),
                   state_spec],
        out_shape=[jax.ShapeDtypeStruct((batch, seq, heads * hd), BF16),
                   jax.ShapeDtypeStruct((batch, heads, hd, hd), F32)],
        scratch_shapes=[pltpu.VMEM((3, tb + SUBLANES, wd), F32),
                        pltpu.VMEM((hg, hd, hd), F32)],
        compiler_params=_params("parallel", "parallel", "arbitrary"),
        name="gated_deltanet",
    )(a_log, dt_bias, z, z, z, z, rows, conv_w8, conv_w8, conv_w8,
      conv_buf8, conv_buf8, conv_buf8, s0, norm_w.reshape(1, hd))


def _attn_prompt_kernel(q_ref, k_ref, v_ref, gate_ref, bias_ref, qw_ref, kw_ref,
                        o_ref, kc_ref, kbuf, vbuf, qs, *, tq):
    i = pl.program_id(2)

    @pl.when(i == 0)
    def _():
        kbuf[0:PREV_ROWS, :] = jnp.zeros((PREV_ROWS, HEAD_DIM), BF16)
        vbuf[0:PREV_ROWS, :] = jnp.zeros((PREV_ROWS, HEAD_DIM), BF16)

    @pl.when(i > 0)
    def _():
        kbuf[0:PREV_ROWS, :] = kbuf[tq:tq + PREV_ROWS, :]
        vbuf[0:PREV_ROWS, :] = vbuf[tq:tq + PREV_ROWS, :]

    kn = _rms(k_ref[0], kw_ref[...])
    kc_ref[0] = kn
    kbuf[PREV_ROWS:PREV_ROWS + tq, :] = kn.astype(BF16)
    vbuf[PREV_ROWS:PREV_ROWS + tq, :] = v_ref[0].astype(BF16)
    qs[...] = (_rms(q_ref[0], qw_ref[...]) * (HEAD_DIM ** -0.5)).astype(BF16)

    bias = bias_ref[0][:, :BAND]
    col = lax.broadcasted_iota(jnp.int32, (CHUNK, BAND), 1)
    for c in range(tq // CHUNK):
        rows = slice(c * CHUNK, (c + 1) * CHUNK)
        band = slice(c * CHUNK, c * CHUNK + BAND)
        s = _dot_nt(qs[rows, :], kbuf[band, :]) + bias
        n_before_start = jnp.where(i == 0, PREV_ROWS - c * CHUNK, 0)
        s = jnp.where(col < n_before_start, -1e30, s)
        p = jnp.exp(s - jnp.max(s, axis=-1, keepdims=True))
        o = _dot(p.astype(BF16), vbuf[band, :]) / jnp.sum(p, axis=-1, keepdims=True)
        o_ref[0, rows, :] = (o * _silu(gate_ref[0, rows, :])).astype(BF16)


def _attn_prompt_call(z, bias_tab, q_norm_w, k_norm_w, heads):
    batch, seq, _ = z.shape
    tq = PREV_ROWS
    assert seq % tq == 0
    hd = HEAD_DIM

    def zcol(off):
        return pl.BlockSpec((1, tq, hd), lambda b, h, i: (b, i, off + h))

    kern = functools.partial(_attn_prompt_kernel, tq=tq)
    return pl.pallas_call(
        kern,
        grid=(batch, heads, seq // tq),
        in_specs=[zcol(4 * heads), zcol(5 * heads), zcol(6 * heads), zcol(7 * heads),
                  pl.BlockSpec((1, CHUNK, BIAS_COLS), lambda b, h, i: (h, 0, 0)),
                  pl.BlockSpec((1, hd), lambda b, h, i: (0, 0)),
                  pl.BlockSpec((1, hd), lambda b, h, i: (0, 0))],
        out_specs=[pl.BlockSpec((1, tq, hd), lambda b, h, i: (b, i, h)),
                   pl.BlockSpec((1, tq, hd), lambda b, h, i: (b, 0, h))],
        out_shape=[jax.ShapeDtypeStruct((batch, seq, heads * hd), BF16),
                   jax.ShapeDtypeStruct((batch, tq, heads * hd), F32)],
        scratch_shapes=[pltpu.VMEM((PREV_ROWS + tq, hd), BF16),
                        pltpu.VMEM((PREV_ROWS + tq, hd), BF16),
                        pltpu.VMEM((tq, hd), BF16)],
        compiler_params=_params("parallel", "parallel", "arbitrary"),
        name="band_attention_prompt",
    )(z, z, z, z, bias_tab, q_norm_w.reshape(1, hd), k_norm_w.reshape(1, hd))


def _attn_sample_kernel(q_ref, k_ref, v_ref, gate_ref, ck_ref, cv_ref, bias_ref, qw_ref, kw_ref,
                        o_ref, kn_ref, *, seq, n_cached):
    kn = _rms(k_ref[0], kw_ref[...])
    kn_ref[0] = kn
    q16 = (_rms(q_ref[0], qw_ref[...]) * (HEAD_DIM ** -0.5)).astype(BF16)
    bias = bias_ref[0]
    s_old = _dot_nt(q16, ck_ref[0].astype(BF16)) + bias[:seq, :n_cached]
    s_new = _dot_nt(q16, kn.astype(BF16)) + bias[:seq, n_cached:n_cached + seq]
    m = jnp.maximum(jnp.max(s_old, axis=-1, keepdims=True), jnp.max(s_new, axis=-1, keepdims=True))
    p_old = jnp.exp(s_old - m)
    p_new = jnp.exp(s_new - m)
    denom = jnp.sum(p_old, axis=-1, keepdims=True) + jnp.sum(p_new, axis=-1, keepdims=True)
    o = (_dot(p_old.astype(BF16), cv_ref[0].astype(BF16))
         + _dot(p_new.astype(BF16), v_ref[0].astype(BF16))) / denom
    o_ref[0] = (o * _silu(gate_ref[0])).astype(BF16)


def _attn_sample_call(z, cache_k, cache_v, bias_tab, q_norm_w, k_norm_w, heads):
    batch, seq, _ = z.shape
    n_cached = cache_k.shape[1]
    hd = HEAD_DIM
    q_pos = PAST_LEN + np.arange(seq)
    k_pos = PAST_LEN - n_cached + np.arange(n_cached + seq)
    qc, kc = q_pos // CHUNK, k_pos // CHUNK
    valid = (k_pos[None] >= 0) & (kc[None] <= qc[:, None]) & (kc[None] >= qc[:, None] - BAND_PREV)
    assert valid.all() and n_cached == PREV_ROWS and seq <= CHUNK and n_cached + seq <= BIAS_COLS

    def zcol(off):
        return pl.BlockSpec((1, seq, hd), lambda b, h: (b, 0, off + h))

    cache_spec = pl.BlockSpec((1, n_cached, hd), lambda b, h: (b, 0, h))
    kern = functools.partial(_attn_sample_kernel, seq=seq, n_cached=n_cached)
    return pl.pallas_call(
        kern,
        grid=(batch, heads),
        in_specs=[zcol(4 * heads), zcol(5 * heads), zcol(6 * heads), zcol(7 * heads),
                  cache_spec, cache_spec,
                  pl.BlockSpec((1, CHUNK, BIAS_COLS), lambda b, h: (h, 0, 0)),
                  pl.BlockSpec((1, hd), lambda b, h: (0, 0)),
                  pl.BlockSpec((1, hd), lambda b, h: (0, 0))],
        out_specs=[pl.BlockSpec((1, seq, hd), lambda b, h: (b, 0, h)),
                   pl.BlockSpec((1, seq, hd), lambda b, h: (b, 0, h))],
        out_shape=[jax.ShapeDtypeStruct((batch, seq, heads * hd), BF16),
                   jax.ShapeDtypeStruct((batch, seq, heads * hd), F32)],
        compiler_params=_params("parallel", "parallel"),
        name="band_attention_sample",
    )(z, z, z, z, cache_k.reshape(batch, n_cached, heads * hd),
      cache_v.reshape(batch, n_cached, heads * hd), bias_tab,
      q_norm_w.reshape(1, hd), k_norm_w.reshape(1, hd))


def _out_kernel(oa_ref, ob_ref, ma_ref, mb_ref, x_ref, g_ref, wa_ref, wb_ref, wo_ref, y_ref):
    bb, tt, d = x_ref.shape
    ya = _dot(oa_ref[...].reshape(bb * tt, -1), wa_ref[...])
    yb = _dot(ob_ref[...].reshape(bb * tt, -1), wb_ref[...])
    merged = (jax.nn.sigmoid(ma_ref[...].reshape(bb * tt, d)) * ya
              + jax.nn.sigmoid(mb_ref[...].reshape(bb * tt, d)) * yb)
    delta = _dot(merged.astype(BF16), wo_ref[...]).reshape(bb, tt, d)
    y_ref[...] = x_ref[...] + g_ref[...] * delta


def _out_call(o_a, o_b, z, x, gate, w_proj_a, w_proj_b, w_out, heads):
    batch, seq, d = x.shape
    bb, tt = _row_tiling(batch, seq, 256)
    wa = o_a.shape[-1]
    wb = o_b.shape[-1]
    m_off = (8 * heads * HEAD_DIM) // d
    assert m_off * d == 8 * heads * HEAD_DIM

    def rows(width, off=0):
        return pl.BlockSpec((bb, tt, width), lambda b, i: (b, i, off))

    def whole(shape):
        return pl.BlockSpec(shape, lambda b, i: (0, 0), pipeline_mode=pl.Buffered(1))

    return pl.pallas_call(
        _out_kernel,
        grid=(batch // bb, seq // tt),
        in_specs=[rows(wa), rows(wb), rows(d, m_off), rows(d, m_off + 1), rows(d),
                  pl.BlockSpec((bb, 1, d), lambda b, i: (b, 0, 0)),
                  whole((wa, d)), whole((wb, d)), whole((d, d))],
        out_specs=rows(d),
        out_shape=jax.ShapeDtypeStruct((batch, seq, d), F32),
        compiler_params=_params("parallel", "parallel"),
        name="out_proj",
    )(o_a, o_b, z, z, x, gate, w_proj_a, w_proj_b, w_out)


def _layer(x, mod, conv_buf, s0, cache_k, cache_v, p, bias_tab, heads, prompt):
    (norm_w, w_main, w_small, conv_w8, a_log, dt_bias, gdn_norm_w,
     q_norm_w, k_norm_w, w_proj_a, w_proj_b, w_out) = p
    batch, seq, d = x.shape
    hd = HEAD_DIM
    shift, scale, gate = [m.reshape(batch, 1, d) for m in jnp.split(mod, 3, axis=-1)]

    z, z_small = _inproj_call(x, scale, shift, norm_w, w_main, w_small)

    logits = jnp.transpose(z_small[..., :2 * heads].reshape(batch, seq, 2, heads), (0, 3, 2, 1))
    rows = jnp.pad(logits, ((0, 0), (0, 0), (0, SUBLANES - 2), (0, 0)))
    conv_buf8 = jnp.pad(conv_buf, ((0, 0), (SUBLANES - (CONV_WIDTH - 1), 0), (0, 0)))
    o_a, s_new = _gdn_call(z, rows, conv_w8, conv_buf8, s0, a_log, dt_bias, gdn_norm_w, heads)

    conv_dim = 3 * heads * hd
    new_buf = z[:, seq - (CONV_WIDTH - 1):, :conv_dim]
    if prompt:
        o_b, k_new = _attn_prompt_call(z, bias_tab, q_norm_w, k_norm_w, heads)
        keep = k_new.shape[1]
        v_new = z[:, seq - keep:, 6 * heads * hd:7 * heads * hd]
    else:
        o_b, k_new = _attn_sample_call(z, cache_k, cache_v, bias_tab, q_norm_w, k_norm_w, heads)
        keep = seq
        v_new = z[:, :, 6 * heads * hd:7 * heads * hd]
    k_new = k_new.reshape(batch, keep, heads, hd)
    v_new = v_new.reshape(batch, keep, heads, hd)

    y = _out_call(o_a, o_b, z, x, gate, w_proj_a, w_proj_b, w_out, heads)
    return y, new_buf, s_new, k_new, v_new


def kernel(x_prompt, x_sample, c_prompt, c_sample, state_conv, state_delta, cache_k, cache_v, norm_w, ada_w, ada_b, w_in, conv_w, gdn_a_log, gdn_dt_bias, gdn_norm_w, q_norm_w, k_norm_w, rel_bias, w_proj_a, w_proj_b, w_out):
    bp, _, d = x_prompt.shape
    depth = w_in.shape[0]
    heads = gdn_a_log.shape[1]
    hd = HEAD_DIM
    small0 = 4 * heads * hd
    assert w_in.shape[2] == 8 * heads * hd + 2 * heads + 2 * d
    assert conv_w.shape[1:] == (CONV_WIDTH, 3 * heads * hd) and rel_bias.shape[1:] == (heads, N_REL)

    mod = _mod_call(jnp.concatenate([c_prompt, c_sample], axis=0), ada_w, ada_b)
    bias_tab = _bias_call(rel_bias)

    yp, ys = x_prompt, x_sample
    outs_p, outs_s = [], []
    zero_buf = jnp.zeros((bp, CONV_WIDTH - 1, 3 * heads * hd), x_prompt.dtype)
    zero_state = jnp.zeros((bp, heads, hd, hd), state_delta.dtype)
    for l in range(depth):
        w = w_in[l]
        w_main = jnp.concatenate([w[:, :small0], w[:, small0 + 2 * heads:]], axis=1).astype(BF16)
        w_small = jnp.pad(w[:, small0:small0 + 2 * heads], ((0, 0), (0, LANES - 2 * heads))).astype(BF16)
        conv_w8 = jnp.pad(conv_w[l], ((0, SUBLANES - CONV_WIDTH), (0, 0)))
        p = (norm_w[l], w_main, w_small, conv_w8, gdn_a_log[l], gdn_dt_bias[l], gdn_norm_w[l],
             q_norm_w[l], k_norm_w[l], w_proj_a[l].astype(BF16), w_proj_b[l].astype(BF16),
             w_out[l].astype(BF16))
        yp, *rest = _layer(yp, mod[l, :bp], zero_buf, zero_state, None, None, p, bias_tab[l], heads, True)
        outs_p.append(rest)
        ys, *rest = _layer(ys, mod[l, bp:], state_conv[l], state_delta[l], cache_k[l], cache_v[l],
                           p, bias_tab[l], heads, False)
        outs_s.append(rest)

    stack = lambda outs, i: jnp.stack([o[i] for o in outs])
    return (yp, ys,
            stack(outs_p, 0), stack(outs_p, 1), stack(outs_p, 2), stack(outs_p, 3),
            stack(outs_s, 0), stack(outs_s, 1), stack(outs_s, 2), stack(outs_s, 3))
```

```python
import functools
import math

import numpy as np
import jax
import jax.numpy as jnp
from jax import lax
from jax.experimental import pallas as pl
from jax.experimental.pallas import tpu as pltpu

F32 = jnp.float32
BF16 = jnp.bfloat16
HIGHEST = lax.Precision.HIGHEST

EPS = 1e-6
CHUNK = 64
BAND_PREV = 8
REL_CLIP = 128
N_REL = 2 * REL_CLIP + 1
PAST_LEN = 2048
CONV_WIDTH = 4
HEAD_DIM = 128
LANES = 128
SUBLANES = 8
VMEM_LIMIT_BYTES = 56 * 1024 * 1024
GDN_BLOCK = 64
GDN_SUPER = 256
GDN_HEADS_PER_STEP = 4
PREV_ROWS = BAND_PREV * CHUNK
BAND = PREV_ROWS + CHUNK
PAIR_ROWS = 2 * CHUNK
PAIR_COLS = PREV_ROWS + PAIR_ROWS
MASKED = -1e30
ATT_HEADS_PER_STEP = 2


def _dot(a, b):
    return jnp.dot(a, b, preferred_element_type=F32)


def _dot_nt(a, b):
    return lax.dot_general(a, b, (((1,), (1,)), ((), ())), preferred_element_type=F32)


def _dot_tn(a, b):
    return lax.dot_general(a, b, (((0,), (0,)), ((), ())), preferred_element_type=F32)


def _silu(x):
    return x * jax.nn.sigmoid(x)


def _rms(x, w):
    return x * lax.rsqrt(jnp.mean(x * x, axis=-1, keepdims=True) + EPS) * w


def _params(*sem):
    return pltpu.CompilerParams(dimension_semantics=sem, vmem_limit_bytes=VMEM_LIMIT_BYTES)


def _largest_divisor(n, candidates):
    for c in candidates:
        if n % c == 0:
            return c
    raise ValueError(f"no tile in {candidates} divides {n}")


def _row_tiling(batch, seq, rows):
    if seq >= rows:
        assert seq % rows == 0
        return 1, rows
    bb = min(batch, rows // seq)
    assert batch % bb == 0
    return bb, seq


def _mod_kernel(c_ref, w_ref, b_ref, o_ref):
    o_ref[0] = jnp.dot(_silu(c_ref[...]), w_ref[0], precision=HIGHEST,
                       preferred_element_type=F32) + b_ref[0]


def _mod_call(c, ada_w, ada_b):
    depth, d, n = ada_w.shape
    rows = c.shape[0]
    tn = _largest_divisor(n, (1024, 512, 256, 128))
    return pl.pallas_call(
        _mod_kernel,
        grid=(depth, n // tn),
        in_specs=[pl.BlockSpec((rows, d), lambda l, j: (0, 0)),
                  pl.BlockSpec((1, d, tn), lambda l, j: (l, 0, j)),
                  pl.BlockSpec((1, 1, tn), lambda l, j: (l, 0, j))],
        out_specs=pl.BlockSpec((1, rows, tn), lambda l, j: (l, 0, j)),
        out_shape=jax.ShapeDtypeStruct((depth, rows, n), F32),
        compiler_params=_params("parallel", "parallel"),
        name="adaln_mod",
    )(c, ada_w, ada_b.reshape(depth, 1, n))


def _bias_kernel(rb_ref, o_ref):
    rb = rb_ref[0]
    heads, nrel = rb.shape
    r = lax.broadcasted_iota(jnp.int32, (nrel, PAIR_COLS), 0)
    j = lax.broadcasted_iota(jnp.int32, (nrel, PAIR_COLS), 1)
    key_chunk = lax.broadcasted_iota(jnp.int32, (heads, PAIR_COLS), 1) // CHUNK

    def body(i, carry):
        dist = jnp.clip(PREV_ROWS + i - j, -REL_CLIP, REL_CLIP) + REL_CLIP
        onehot = (dist == r).astype(F32)
        row = jnp.dot(rb, onehot, precision=HIGHEST, preferred_element_type=F32)
        q_chunk = i // CHUNK
        in_band = (key_chunk >= q_chunk) & (key_chunk <= q_chunk + BAND_PREV)
        o_ref[0, i] = jnp.where(in_band, row, MASKED)
        return carry

    lax.fori_loop(0, PAIR_ROWS, body, 0)


def _bias_call(rel_bias):
    depth, heads, nrel = rel_bias.shape
    nrel_pad = -(-nrel // SUBLANES) * SUBLANES
    rb = jnp.pad(rel_bias, ((0, 0), (0, 0), (0, nrel_pad - nrel)))
    tab = pl.pallas_call(
        _bias_kernel,
        grid=(depth,),
        in_specs=[pl.BlockSpec((1, heads, nrel_pad), lambda l: (l, 0, 0))],
        out_specs=pl.BlockSpec((1, PAIR_ROWS, heads, PAIR_COLS), lambda l: (l, 0, 0, 0)),
        out_shape=jax.ShapeDtypeStruct((depth, PAIR_ROWS, heads, PAIR_COLS), F32),
        compiler_params=_params("parallel"),
        name="rel_bias_table",
    )(rb)
    return jnp.transpose(tab, (0, 2, 1, 3))


def _inproj_kernel(x_ref, sc_ref, sh_ref, nw_ref, w_ref, ws_ref, z_ref, zs_ref, h_ref, *, bb, tt, rc):
    @pl.when(pl.program_id(2) == 0)
    def _():
        for b in range(bb):
            for r0 in range(0, tt, rc):
                x = x_ref[b, r0:r0 + rc, :]
                h = _rms(x, nw_ref[...]) * (1.0 + sc_ref[b]) + sh_ref[b]
                h_ref[b * tt + r0:b * tt + r0 + rc, :] = h.astype(BF16)
        zs_ref[...] = _dot(h_ref[...], ws_ref[...]).reshape(zs_ref.shape)

    z_ref[...] = _dot(h_ref[...], w_ref[...]).astype(z_ref.dtype).reshape(z_ref.shape)


def _inproj_call(x, scale, shift, norm_w, w_main, w_small):
    batch, seq, d = x.shape
    n = w_main.shape[1]
    ns = w_small.shape[1]
    bb, tt = _row_tiling(batch, seq, 1024)
    tn = _largest_divisor(n, (2048, 1024, 512, 256, 128))
    kern = functools.partial(_inproj_kernel, bb=bb, tt=tt, rc=min(tt, 256))
    return pl.pallas_call(
        kern,
        grid=(batch // bb, seq // tt, n // tn),
        in_specs=[pl.BlockSpec((bb, tt, d), lambda b, i, j: (b, i, 0)),
                  pl.BlockSpec((bb, 1, d), lambda b, i, j: (b, 0, 0)),
                  pl.BlockSpec((bb, 1, d), lambda b, i, j: (b, 0, 0)),
                  pl.BlockSpec((1, d), lambda b, i, j: (0, 0)),
                  pl.BlockSpec((d, tn), lambda b, i, j: (0, j)),
                  pl.BlockSpec((d, ns), lambda b, i, j: (0, 0))],
        out_specs=[pl.BlockSpec((bb, tt, tn), lambda b, i, j: (b, i, j)),
                   pl.BlockSpec((bb, tt, ns), lambda b, i, j: (b, i, 0))],
        out_shape=[jax.ShapeDtypeStruct((batch, seq, n), BF16),
                   jax.ShapeDtypeStruct((batch, seq, ns), F32)],
        scratch_shapes=[pltpu.VMEM((bb * tt, d), BF16)],
        compiler_params=_params("parallel", "parallel", "arbitrary"),
        name="in_proj",
    )(x, scale, shift, norm_w.reshape(1, d), w_main, w_small)


def _gdn_kernel(alog_ref, dt_ref, q_ref, k_ref, v_ref, gate_ref, r_ref,
                cwq_ref, cwk_ref, cwv_ref, csq_ref, csk_ref, csv_ref, s0_ref, nw_ref,
                o_ref, sfin_ref, xs_ref, s_ref, *, tb, blk, hg):
    head0 = pl.program_id(1) * hg
    t = pl.program_id(2)
    pad = SUBLANES
    taps = CONV_WIDTH
    hd = HEAD_DIM
    heads = range(hg)

    @pl.when(t == 0)
    def _():
        xs_ref[0, 0:pad, :] = csq_ref[0]
        xs_ref[1, 0:pad, :] = csk_ref[0]
        xs_ref[2, 0:pad, :] = csv_ref[0]
        s_ref[...] = s0_ref[0]

    xs_ref[0, pad:pad + tb, :] = q_ref[0].astype(F32)
    xs_ref[1, pad:pad + tb, :] = k_ref[0].astype(F32)
    xs_ref[2, pad:pad + tb, :] = v_ref[0].astype(F32)

    def conv_act(i, cw_ref, h):
        cols = slice(h * hd, (h + 1) * hd)
        w = cw_ref[:, cols]
        base = pad - (taps - 1)
        acc = xs_ref[i, base:base + tb, cols] * w[0:1]
        for j in range(1, taps):
            acc = acc + xs_ref[i, base + j:base + j + tb, cols] * w[j:j + 1]
        return _silu(acc)

    shift = int(math.log2(blk))
    ri = lax.broadcasted_iota(jnp.int32, (tb, tb), 0)
    ci = lax.broadcasted_iota(jnp.int32, (tb, tb), 1)
    same = (ri >> shift) == (ci >> shift)
    incl = same & (ri >= ci)
    strict = same & (ri > ci)
    sum_mats = jnp.concatenate([(same & (ri <= ci)).astype(F32), same.astype(F32)], axis=1)
    eye = (ri == ci).astype(F32)
    row_id = lax.broadcasted_iota(jnp.int32, (SUBLANES, tb), 0)

    def split16(x):
        hi = x.astype(BF16)
        return hi, (x - hi.astype(F32)).astype(BF16)

    q, k, v, neg_m, a_intra, beta_c, g_c, g_last_c = [], [], [], [], [], [], [], []
    for h in heads:
        aq = conv_act(0, cwq_ref, h)
        ak = conv_act(1, cwk_ref, h)
        v.append(conv_act(2, cwv_ref, h))
        q.append(aq * lax.rsqrt(jnp.sum(aq * aq, axis=-1, keepdims=True) + EPS) * (hd ** -0.5))
        k.append(ak * lax.rsqrt(jnp.sum(ak * ak, axis=-1, keepdims=True) + EPS))

        r = r_ref[0, h]
        beta_rows = jax.nn.sigmoid(r)
        xx = r + jnp.full(r.shape, dt_ref[head0 + h], F32)
        softplus = jnp.maximum(xx, 0.0) + jnp.log1p(jnp.exp(-jnp.abs(xx)))
        g_rows = -jnp.exp(jnp.full(r.shape, alog_ref[head0 + h], F32)) * softplus
        sums = jnp.dot(g_rows, sum_mats, precision=HIGHEST, preferred_element_type=F32)
        g_cum_rows = sums[:, :tb]
        cols = jnp.transpose(jnp.where(row_id == 0, beta_rows, g_cum_rows))
        beta_c.append(cols[:, 0:1])
        g_c.append(cols[:, 1:2])
        g_last_c.append(jnp.transpose(sums[:, tb:])[:, 1:2])
        decay = jnp.where(incl, jnp.exp(jnp.where(incl, g_c[h] - g_cum_rows[1:2, :], 0.0)), 0.0)

        k16 = k[h].astype(BF16)
        kk = _dot_nt(k16, k16)
        qk = _dot_nt(q[h].astype(BF16), k16)
        neg_m.append(jnp.where(strict, -(beta_c[h] * kk) * decay, 0.0))
        a_intra.append((qk * decay).astype(BF16))

    for i in range(3):
        xs_ref[i, 0:pad, :] = xs_ref[i, tb:tb + pad, :]

    n_split = [split16(m) for m in neg_m]
    x = [eye + m for m in neg_m]
    steps = shift - 1
    for step in range(steps):
        if step < steps - 1:
            nx = [_dot(n_split[h][0], x[h].astype(BF16)) for h in heads]
        else:
            xs16 = [split16(x[h]) for h in heads]
            nx = [_dot(n_split[h][0], xs16[h][0])
                  + (_dot(n_split[h][0], xs16[h][1]) + _dot(n_split[h][1], xs16[h][0])) for h in heads]
        resid = [((eye - x[h]) + nx[h]).astype(BF16) for h in heads]
        x = [x[h] + _dot(x[h].astype(BF16), resid[h]) for h in heads]

    u, w, qg, kg, e_last = [], [], [], [], []
    for h in heads:
        e_g = jnp.exp(g_c[h])
        rhs = jnp.concatenate([v[h] * beta_c[h], k[h] * (beta_c[h] * e_g)], axis=1).astype(BF16)
        uw = _dot(x[h].astype(BF16), rhs)
        u.append(uw[:, :hd])
        w.append(uw[:, hd:])
        qg.append(q[h] * e_g)
        kg.append((k[h] * jnp.exp(g_last_c[h] - g_c[h])).astype(BF16))
        e_last.append(jnp.exp(g_last_c[h]))

    s = [s_ref[h] for h in heads]
    v_new = [[] for _ in heads]
    o_inter = [[] for _ in heads]
    for b in range(tb // blk):
        rs = slice(b * blk, (b + 1) * blk)
        for h in heads:
            ws_qs = _dot(jnp.concatenate([w[h][rs], qg[h][rs]], axis=0).astype(BF16), s[h].astype(BF16))
            vn = u[h][rs] - ws_qs[:blk]
            v_new[h].append(vn)
            o_inter[h].append(ws_qs[blk:])
            s[h] = s[h] * e_last[h][b * blk:b * blk + 1, :] + _dot_tn(kg[h][rs], vn.astype(BF16))

    for h in heads:
        s_ref[h] = s[h]
        cols = slice(h * hd, (h + 1) * hd)
        vn = jnp.concatenate(v_new[h], axis=0).astype(BF16)
        o = jnp.concatenate(o_inter[h], axis=0) + _dot(a_intra[h], vn)
        o_ref[0, :, cols] = (_rms(o, nw_ref[...]) * _silu(gate_ref[0, :, cols].astype(F32))).astype(BF16)

    @pl.when(t == pl.num_programs(2) - 1)
    def _():
        sfin_ref[0] = s_ref[...]


def _gdn_call(z, rows, conv_w8, conv_buf8, s0, a_log, dt_bias, norm_w, heads):
    batch, seq, _ = z.shape
    tb = min(seq, GDN_SUPER)
    blk = min(tb, GDN_BLOCK)
    hg = min(heads, GDN_HEADS_PER_STEP)
    assert seq % tb == 0 and tb % blk == 0 and heads % hg == 0
    hd = HEAD_DIM
    wd = hg * hd
    ng = heads // hg
    smem = pl.BlockSpec(memory_space=pltpu.SMEM)

    def zcol(seg):
        return pl.BlockSpec((1, tb, wd), lambda b, g, t: (b, t, seg * ng + g))

    def cw(seg):
        return pl.BlockSpec((SUBLANES, wd), lambda b, g, t: (0, seg * ng + g))

    def cs(seg):
        return pl.BlockSpec((1, SUBLANES, wd), lambda b, g, t: (b, 0, seg * ng + g))

    state_spec = pl.BlockSpec((1, hg, hd, hd), lambda b, g, t: (b, g, 0, 0))
    kern = functools.partial(_gdn_kernel, tb=tb, blk=blk, hg=hg)
    return pl.pallas_call(
        kern,
        grid=(batch, ng, seq // tb),
        in_specs=[smem, smem,
                  zcol(0), zcol(1), zcol(2), zcol(3),
                  pl.BlockSpec((1, hg, SUBLANES, tb), lambda b, g, t: (b, g, 0, t)),
                  cw(0), cw(1), cw(2),
                  cs(0), cs(1), cs(2),
                  state_spec,
                  pl.BlockSpec((1, hd), lambda b, g, t: (0, 0))],
        out_specs=[pl.BlockSpec((1, tb, wd), lambda b, g, t: (b, t, g)),
                   state_spec],
        out_shape=[jax.ShapeDtypeStruct((batch, seq, heads * hd), BF16),
                   jax.ShapeDtypeStruct((batch, heads, hd, hd), F32)],
        scratch_shapes=[pltpu.VMEM((3, tb + SUBLANES, wd), F32),
                        pltpu.VMEM((hg, hd, hd), F32)],
        compiler_params=_params("parallel", "parallel", "arbitrary"),
        name="gated_deltanet",
    )(a_log, dt_bias, z, z, z, z, rows, conv_w8, conv_w8, conv_w8,
      conv_buf8, conv_buf8, conv_buf8, s0, norm_w.reshape(1, hd))


def _attn_prompt_kernel(q_ref, k_ref, v_ref, gate_ref, bias_ref, qw_ref, kw_ref,
                        o_ref, kc_ref, kbuf, vbuf, qs, bias_s, *, tq, hg):
    i = pl.program_id(2)
    hd = HEAD_DIM
    npairs = tq // PAIR_ROWS
    problems = [(h, p) for h in range(hg) for p in range(npairs)]

    for h in range(hg):
        cols = slice(h * hd, (h + 1) * hd)

        @pl.when(i == 0)
        def _():
            kbuf[h, 0:PREV_ROWS, :] = jnp.zeros((PREV_ROWS, hd), BF16)
            vbuf[h, 0:PREV_ROWS, :] = jnp.zeros((PREV_ROWS, hd), BF16)

        @pl.when(i > 0)
        def _():
            kbuf[h, 0:PREV_ROWS, :] = kbuf[h, tq:tq + PREV_ROWS, :]
            vbuf[h, 0:PREV_ROWS, :] = vbuf[h, tq:tq + PREV_ROWS, :]

        kn = _rms(k_ref[0, :, cols].astype(F32), kw_ref[...])
        kc_ref[0, :, cols] = kn
        kbuf[h, PREV_ROWS:PREV_ROWS + tq, :] = kn.astype(BF16)
        vbuf[h, PREV_ROWS:PREV_ROWS + tq, :] = v_ref[0, :, cols]
        qs[h] = (_rms(q_ref[0, :, cols].astype(F32), qw_ref[...]) * (hd ** -0.5)).astype(BF16)

    @pl.when(i == 0)
    def _():
        col = lax.broadcasted_iota(jnp.int32, (PAIR_ROWS, PAIR_COLS), 1)
        for h, p in problems:
            bias_s[h, p] = jnp.where(col < PREV_ROWS - p * PAIR_ROWS, MASKED, bias_ref[h])

    @pl.when(i == 1)
    def _():
        for h, p in problems:
            bias_s[h, p] = bias_ref[h]

    def window(p):
        return slice(p * PAIR_ROWS, p * PAIR_ROWS + PAIR_COLS)

    def rows(p):
        return slice(p * PAIR_ROWS, (p + 1) * PAIR_ROWS)

    s = [_dot_nt(qs[h, rows(p), :], kbuf[h, window(p), :]) + bias_s[h, p] for h, p in problems]
    e = [jnp.exp(x - jnp.max(x, axis=-1, keepdims=True)) for x in s]
    denom = [jnp.sum(x, axis=-1, keepdims=True) for x in e]
    for n, (h, p) in enumerate(problems):
        cols = slice(h * hd, (h + 1) * hd)
        o = _dot(e[n].astype(BF16), vbuf[h, window(p), :]) / denom[n]
        o_ref[0, rows(p), cols] = (o * _silu(gate_ref[0, rows(p), cols].astype(F32))).astype(BF16)


def _attn_prompt_call(z, bias_tab, q_norm_w, k_norm_w, heads):
    batch, seq, _ = z.shape
    tq = PREV_ROWS
    hg = min(heads, ATT_HEADS_PER_STEP)
    assert seq % tq == 0 and tq % PAIR_ROWS == 0 and heads % hg == 0
    hd = HEAD_DIM
    wd = hg * hd
    ng = heads // hg

    def zcol(seg):
        return pl.BlockSpec((1, tq, wd), lambda b, g, i: (b, i, seg * ng + g))

    kern = functools.partial(_attn_prompt_kernel, tq=tq, hg=hg)
    return pl.pallas_call(
        kern,
        grid=(batch, ng, seq // tq),
        in_specs=[zcol(4), zcol(5), zcol(6), zcol(7),
                  pl.BlockSpec((hg, PAIR_ROWS, PAIR_COLS), lambda b, g, i: (g, 0, 0)),
                  pl.BlockSpec((1, hd), lambda b, g, i: (0, 0)),
                  pl.BlockSpec((1, hd), lambda b, g, i: (0, 0))],
        out_specs=[pl.BlockSpec((1, tq, wd), lambda b, g, i: (b, i, g)),
                   pl.BlockSpec((1, tq, wd), lambda b, g, i: (b, 0, g))],
        out_shape=[jax.ShapeDtypeStruct((batch, seq, heads * hd), BF16),
                   jax.ShapeDtypeStruct((batch, tq, heads * hd), F32)],
        scratch_shapes=[pltpu.VMEM((hg, PREV_ROWS + tq, hd), BF16),
                        pltpu.VMEM((hg, PREV_ROWS + tq, hd), BF16),
                        pltpu.VMEM((hg, tq, hd), BF16),
                        pltpu.VMEM((hg, tq // PAIR_ROWS, PAIR_ROWS, PAIR_COLS), F32)],
        compiler_params=_params("parallel", "parallel", "arbitrary"),
        name="band_attention_prompt",
    )(z, z, z, z, bias_tab, q_norm_w.reshape(1, hd), k_norm_w.reshape(1, hd))


def _attn_sample_kernel(q_ref, k_ref, v_ref, gate_ref, ck_ref, cv_ref, bias_ref, qw_ref, kw_ref,
                        o_ref, kn_ref, *, seq, n_cached):
    kn = _rms(k_ref[0].astype(F32), kw_ref[...])
    kn_ref[0] = kn
    q16 = (_rms(q_ref[0].astype(F32), qw_ref[...]) * (HEAD_DIM ** -0.5)).astype(BF16)
    bias = bias_ref[0]
    s_old = _dot_nt(q16, ck_ref[0].astype(BF16)) + bias[:seq, :n_cached]
    s_new = _dot_nt(q16, kn.astype(BF16)) + bias[:seq, n_cached:n_cached + seq]
    m = jnp.maximum(jnp.max(s_old, axis=-1, keepdims=True), jnp.max(s_new, axis=-1, keepdims=True))
    p_old = jnp.exp(s_old - m)
    p_new = jnp.exp(s_new - m)
    denom = jnp.sum(p_old, axis=-1, keepdims=True) + jnp.sum(p_new, axis=-1, keepdims=True)
    o = (_dot(p_old.astype(BF16), cv_ref[0].astype(BF16))
         + _dot(p_new.astype(BF16), v_ref[0])) / denom
    o_ref[0] = (o * _silu(gate_ref[0].astype(F32))).astype(BF16)


def _attn_sample_call(z, cache_k, cache_v, bias_tab, q_norm_w, k_norm_w, heads):
    batch, seq, _ = z.shape
    n_cached = cache_k.shape[1]
    hd = HEAD_DIM
    q_pos = PAST_LEN + np.arange(seq)
    k_pos = PAST_LEN - n_cached + np.arange(n_cached + seq)
    qc, kc = q_pos // CHUNK, k_pos // CHUNK
    valid = (k_pos[None] >= 0) & (kc[None] <= qc[:, None]) & (kc[None] >= qc[:, None] - BAND_PREV)
    assert valid.all() and n_cached == PREV_ROWS and seq <= CHUNK and n_cached + seq <= BAND

    def zcol(off):
        return pl.BlockSpec((1, seq, hd), lambda b, h: (b, 0, off + h))

    cache_spec = pl.BlockSpec((1, n_cached, hd), lambda b, h: (b, 0, h))
    kern = functools.partial(_attn_sample_kernel, seq=seq, n_cached=n_cached)
    return pl.pallas_call(
        kern,
        grid=(batch, heads),
        in_specs=[zcol(4 * heads), zcol(5 * heads), zcol(6 * heads), zcol(7 * heads),
                  cache_spec, cache_spec,
                  pl.BlockSpec((1, PAIR_ROWS, PAIR_COLS), lambda b, h: (h, 0, 0)),
                  pl.BlockSpec((1, hd), lambda b, h: (0, 0)),
                  pl.BlockSpec((1, hd), lambda b, h: (0, 0))],
        out_specs=[pl.BlockSpec((1, seq, hd), lambda b, h: (b, 0, h)),
                   pl.BlockSpec((1, seq, hd), lambda b, h: (b, 0, h))],
        out_shape=[jax.ShapeDtypeStruct((batch, seq, heads * hd), BF16),
                   jax.ShapeDtypeStruct((batch, seq, heads * hd), F32)],
        compiler_params=_params("parallel", "parallel"),
        name="band_attention_sample",
    )(z, z, z, z, cache_k.reshape(batch, n_cached, heads * hd),
      cache_v.reshape(batch, n_cached, heads * hd), bias_tab,
      q_norm_w.reshape(1, hd), k_norm_w.reshape(1, hd))


def _out_kernel(oa_ref, ob_ref, ma_ref, mb_ref, x_ref, g_ref, wa_ref, wb_ref, wo_ref, y_ref):
    bb, tt, d = x_ref.shape
    ya = _dot(oa_ref[...].reshape(bb * tt, -1), wa_ref[...])
    yb = _dot(ob_ref[...].reshape(bb * tt, -1), wb_ref[...])
    merged = (jax.nn.sigmoid(ma_ref[...].reshape(bb * tt, d).astype(F32)) * ya
              + jax.nn.sigmoid(mb_ref[...].reshape(bb * tt, d).astype(F32)) * yb)
    delta = _dot(merged.astype(BF16), wo_ref[...]).reshape(bb, tt, d)
    y_ref[...] = x_ref[...] + g_ref[...] * delta


def _out_call(o_a, o_b, z, x, gate, w_proj_a, w_proj_b, w_out, heads):
    batch, seq, d = x.shape
    bb, tt = _row_tiling(batch, seq, 512)
    wa = o_a.shape[-1]
    wb = o_b.shape[-1]
    m_off = (8 * heads * HEAD_DIM) // d
    assert m_off * d == 8 * heads * HEAD_DIM

    def rows(width, off=0):
        return pl.BlockSpec((bb, tt, width), lambda b, i: (b, i, off))

    def whole(shape):
        return pl.BlockSpec(shape, lambda b, i: (0, 0), pipeline_mode=pl.Buffered(1))

    return pl.pallas_call(
        _out_kernel,
        grid=(batch // bb, seq // tt),
        in_specs=[rows(wa), rows(wb), rows(d, m_off), rows(d, m_off + 1), rows(d),
                  pl.BlockSpec((bb, 1, d), lambda b, i: (b, 0, 0)),
                  whole((wa, d)), whole((wb, d)), whole((d, d))],
        out_specs=rows(d),
        out_shape=jax.ShapeDtypeStruct((batch, seq, d), F32),
        compiler_params=_params("parallel", "parallel"),
        name="out_proj",
    )(o_a, o_b, z, z, x, gate, w_proj_a, w_proj_b, w_out)


def _layer(x, mod, conv_buf, s0, cache_k, cache_v, p, bias_tab, heads, prompt):
    (norm_w, w_main, w_small, conv_w8, a_log, dt_bias, gdn_norm_w,
     q_norm_w, k_norm_w, w_proj_a, w_proj_b, w_out) = p
    batch, seq, d = x.shape
    hd = HEAD_DIM
    shift, scale, gate = [m.reshape(batch, 1, d) for m in jnp.split(mod, 3, axis=-1)]

    z, z_small = _inproj_call(x, scale, shift, norm_w, w_main, w_small)

    logits = jnp.transpose(z_small[..., :2 * heads].reshape(batch, seq, 2, heads), (0, 3, 2, 1))
    rows = jnp.pad(logits, ((0, 0), (0, 0), (0, SUBLANES - 2), (0, 0)))
    conv_buf8 = jnp.pad(conv_buf, ((0, 0), (SUBLANES - (CONV_WIDTH - 1), 0), (0, 0)))
    o_a, s_new = _gdn_call(z, rows, conv_w8, conv_buf8, s0, a_log, dt_bias, gdn_norm_w, heads)

    conv_dim = 3 * heads * hd
    new_buf = z[:, seq - (CONV_WIDTH - 1):, :conv_dim].astype(F32)
    if prompt:
        o_b, k_new = _attn_prompt_call(z, bias_tab, q_norm_w, k_norm_w, heads)
        keep = k_new.shape[1]
        v_new = z[:, seq - keep:, 6 * heads * hd:7 * heads * hd]
    else:
        o_b, k_new = _attn_sample_call(z, cache_k, cache_v, bias_tab, q_norm_w, k_norm_w, heads)
        keep = seq
        v_new = z[:, :, 6 * heads * hd:7 * heads * hd]
    k_new = k_new.reshape(batch, keep, heads, hd)
    v_new = v_new.reshape(batch, keep, heads, hd).astype(F32)

    y = _out_call(o_a, o_b, z, x, gate, w_proj_a, w_proj_b, w_out, heads)
    return y, new_buf, s_new, k_new, v_new


def kernel(x_prompt, x_sample, c_prompt, c_sample, state_conv, state_delta, cache_k, cache_v, norm_w, ada_w, ada_b, w_in, conv_w, gdn_a_log, gdn_dt_bias, gdn_norm_w, q_norm_w, k_norm_w, rel_bias, w_proj_a, w_proj_b, w_out):
    bp, _, d = x_prompt.shape
    depth = w_in.shape[0]
    heads = gdn_a_log.shape[1]
    hd = HEAD_DIM
    small0 = 4 * heads * hd
    assert w_in.shape[2] == 8 * heads * hd + 2 * heads + 2 * d
    assert conv_w.shape[1:] == (CONV_WIDTH, 3 * heads * hd) and rel_bias.shape[1:] == (heads, N_REL)

    mod = _mod_call(jnp.concatenate([c_prompt, c_sample], axis=0), ada_w, ada_b)
    bias_tab = _bias_call(rel_bias)

    yp, ys = x_prompt, x_sample
    outs_p, outs_s = [], []
    zero_buf = jnp.zeros((bp, CONV_WIDTH - 1, 3 * heads * hd), x_prompt.dtype)
    zero_state = jnp.zeros((bp, heads, hd, hd), state_delta.dtype)
    for l in range(depth):
        w = w_in[l]
        w_main = jnp.concatenate([w[:, :small0], w[:, small0 + 2 * heads:]], axis=1).astype(BF16)
        w_small = jnp.pad(w[:, small0:small0 + 2 * heads], ((0, 0), (0, LANES - 2 * heads))).astype(BF16)
        conv_w8 = jnp.pad(conv_w[l], ((0, SUBLANES - CONV_WIDTH), (0, 0)))
        p = (norm_w[l], w_main, w_small, conv_w8, gdn_a_log[l], gdn_dt_bias[l], gdn_norm_w[l],
             q_norm_w[l], k_norm_w[l], w_proj_a[l].astype(BF16), w_proj_b[l].astype(BF16),
             w_out[l].astype(BF16))
        yp, *rest = _layer(yp, mod[l, :bp], zero_buf, zero_state, None, None, p, bias_tab[l], heads, True)
        outs_p.append(rest)
        ys, *rest = _layer(ys, mod[l, bp:], state_conv[l], state_delta[l], cache_k[l], cache_v[l],
                           p, bias_tab[l], heads, False)
        outs_s.append(rest)

    stack = lambda outs, i: jnp.stack([o[i] for o in outs])
    return (yp, ys,
            stack(outs_p, 0), stack(outs_p, 1), stack(outs_p, 2), stack(outs_p, 3),
            stack(outs_s, 0), stack(outs_s, 1), stack(outs_s, 2), stack(outs_s, 3))
```

```python
import functools
import math

import numpy as np
import jax
import jax.numpy as jnp
from jax import lax
from jax.experimental import pallas as pl
from jax.experimental.pallas import tpu as pltpu

F32 = jnp.float32
BF16 = jnp.bfloat16
HIGHEST = lax.Precision.HIGHEST

EPS = 1e-6
CHUNK = 64
BAND_PREV = 8
REL_CLIP = 128
N_REL = 2 * REL_CLIP + 1
PAST_LEN = 2048
CONV_WIDTH = 4
HEAD_DIM = 128
LANES = 128
SUBLANES = 8
VMEM_LIMIT_BYTES = 56 * 1024 * 1024
GDN_BLOCK = 64
GDN_SUPER = 256
GDN_HEADS_PER_STEP = 8
PREV_ROWS = BAND_PREV * CHUNK
BAND = PREV_ROWS + CHUNK
PAIR_ROWS = 2 * CHUNK
PAIR_COLS = PREV_ROWS + PAIR_ROWS
MASKED = -1e30
ATT_HEADS_PER_STEP = 2


def _dot(a, b):
    return jnp.dot(a, b, preferred_element_type=F32)


def _dot_nt(a, b):
    return lax.dot_general(a, b, (((1,), (1,)), ((), ())), preferred_element_type=F32)


def _dot_tn(a, b):
    return lax.dot_general(a, b, (((0,), (0,)), ((), ())), preferred_element_type=F32)


def _sigmoid(x):
    return 0.5 + 0.5 * jnp.tanh(0.5 * x)


def _silu(x):
    half = 0.5 * x
    return half + half * jnp.tanh(half)


def _rms(x, w):
    return x * lax.rsqrt(jnp.mean(x * x, axis=-1, keepdims=True) + EPS) * w


def _params(*sem):
    return pltpu.CompilerParams(dimension_semantics=sem, vmem_limit_bytes=VMEM_LIMIT_BYTES)


def _largest_divisor(n, candidates):
    for c in candidates:
        if n % c == 0:
            return c
    raise ValueError(f"no tile in {candidates} divides {n}")


def _row_tiling(batch, seq, rows):
    if seq >= rows:
        assert seq % rows == 0
        return 1, rows
    bb = min(batch, rows // seq)
    assert batch % bb == 0
    return bb, seq


def _mod_kernel(c_ref, w_ref, b_ref, o_ref):
    o_ref[0] = _dot(_silu(c_ref[...]).astype(BF16), w_ref[0].astype(BF16)) + b_ref[0]


def _mod_call(c, ada_w, ada_b):
    depth, d, n = ada_w.shape
    rows = c.shape[0]
    tn = _largest_divisor(n, (1024, 512, 256, 128))
    return pl.pallas_call(
        _mod_kernel,
        grid=(depth, n // tn),
        in_specs=[pl.BlockSpec((rows, d), lambda l, j: (0, 0)),
                  pl.BlockSpec((1, d, tn), lambda l, j: (l, 0, j)),
                  pl.BlockSpec((1, 1, tn), lambda l, j: (l, 0, j))],
        out_specs=pl.BlockSpec((1, rows, tn), lambda l, j: (l, 0, j)),
        out_shape=jax.ShapeDtypeStruct((depth, rows, n), F32),
        compiler_params=_params("parallel", "parallel"),
        name="adaln_mod",
    )(c, ada_w, ada_b.reshape(depth, 1, n))


def _bias_kernel(rb_ref, o_ref):
    rb = rb_ref[0]
    heads, nrel = rb.shape
    rb_hi = rb.astype(BF16)
    rest = rb - rb_hi.astype(F32)
    rb_mid = rest.astype(BF16)
    rb_lo = (rest - rb_mid.astype(F32)).astype(BF16)
    r = lax.broadcasted_iota(jnp.int32, (nrel, PAIR_COLS), 0)
    j = lax.broadcasted_iota(jnp.int32, (nrel, PAIR_COLS), 1)
    key_chunk = lax.broadcasted_iota(jnp.int32, (heads, PAIR_COLS), 1) // CHUNK

    def body(i, carry):
        dist = jnp.clip(PREV_ROWS + i - j, -REL_CLIP, REL_CLIP) + REL_CLIP
        onehot = (dist == r).astype(BF16)
        row = (_dot(rb_hi, onehot) + _dot(rb_mid, onehot)) + _dot(rb_lo, onehot)
        q_chunk = i // CHUNK
        in_band = (key_chunk >= q_chunk) & (key_chunk <= q_chunk + BAND_PREV)
        o_ref[0, i] = jnp.where(in_band, row, MASKED)
        return carry

    lax.fori_loop(0, PAIR_ROWS, body, 0)


def _bias_call(rel_bias):
    depth, heads, nrel = rel_bias.shape
    nrel_pad = -(-nrel // SUBLANES) * SUBLANES
    rb = jnp.pad(rel_bias, ((0, 0), (0, 0), (0, nrel_pad - nrel)))
    tab = pl.pallas_call(
        _bias_kernel,
        grid=(depth,),
        in_specs=[pl.BlockSpec((1, heads, nrel_pad), lambda l: (l, 0, 0))],
        out_specs=pl.BlockSpec((1, PAIR_ROWS, heads, PAIR_COLS), lambda l: (l, 0, 0, 0)),
        out_shape=jax.ShapeDtypeStruct((depth, PAIR_ROWS, heads, PAIR_COLS), F32),
        compiler_params=_params("parallel"),
        name="rel_bias_table",
    )(rb)
    return jnp.transpose(tab, (0, 2, 1, 3))


def _inproj_kernel(x_ref, sc_ref, sh_ref, nw_ref, w_ref, ws_ref, z_ref, zs_ref, h_ref, *, bb, tt, rc):
    @pl.when(pl.program_id(2) == 0)
    def _():
        for b in range(bb):
            for r0 in range(0, tt, rc):
                x = x_ref[b, r0:r0 + rc, :]
                h = _rms(x, nw_ref[...]) * (1.0 + sc_ref[b]) + sh_ref[b]
                h_ref[b * tt + r0:b * tt + r0 + rc, :] = h.astype(BF16)
        zs_ref[...] = _dot(h_ref[...], ws_ref[...]).reshape(zs_ref.shape)

    z_ref[...] = _dot(h_ref[...], w_ref[...]).astype(z_ref.dtype).reshape(z_ref.shape)


def _inproj_call(x, scale, shift, norm_w, w_main, w_small):
    batch, seq, d = x.shape
    n = w_main.shape[1]
    ns = w_small.shape[1]
    bb, tt = _row_tiling(batch, seq, 1024)
    tn = _largest_divisor(n, (2048, 1024, 512, 256, 128))
    kern = functools.partial(_inproj_kernel, bb=bb, tt=tt, rc=min(tt, 256))
    return pl.pallas_call(
        kern,
        grid=(batch // bb, seq // tt, n // tn),
        in_specs=[pl.BlockSpec((bb, tt, d), lambda b, i, j: (b, i, 0)),
                  pl.BlockSpec((bb, 1, d), lambda b, i, j: (b, 0, 0)),
                  pl.BlockSpec((bb, 1, d), lambda b, i, j: (b, 0, 0)),
                  pl.BlockSpec((1, d), lambda b, i, j: (0, 0)),
                  pl.BlockSpec((d, tn), lambda b, i, j: (0, j)),
                  pl.BlockSpec((d, ns), lambda b, i, j: (0, 0))],
        out_specs=[pl.BlockSpec((bb, tt, tn), lambda b, i, j: (b, i, j)),
                   pl.BlockSpec((bb, tt, ns), lambda b, i, j: (b, i, 0))],
        out_shape=[jax.ShapeDtypeStruct((batch, seq, n), BF16),
                   jax.ShapeDtypeStruct((batch, seq, ns), F32)],
        scratch_shapes=[pltpu.VMEM((bb * tt, d), BF16)],
        compiler_params=_params("parallel", "parallel", "arbitrary"),
        name="in_proj",
    )(x, scale, shift, norm_w.reshape(1, d), w_main, w_small)


def _gdn_kernel(alog_ref, dt_ref, q_ref, k_ref, v_ref, gate_ref, r_ref,
                cwq_ref, cwk_ref, cwv_ref, csq_ref, csk_ref, csv_ref, s0_ref, nw_ref,
                o_ref, sfin_ref, xs_ref, s_ref, *, tb, blk, hg):
    head0 = pl.program_id(1) * hg
    t = pl.program_id(2)
    pad = SUBLANES
    taps = CONV_WIDTH
    hd = HEAD_DIM
    heads = range(hg)

    @pl.when(t == 0)
    def _():
        xs_ref[0, 0:pad, :] = csq_ref[0]
        xs_ref[1, 0:pad, :] = csk_ref[0]
        xs_ref[2, 0:pad, :] = csv_ref[0]
        s_ref[...] = s0_ref[0]

    xs_ref[0, pad:pad + tb, :] = q_ref[0].astype(F32)
    xs_ref[1, pad:pad + tb, :] = k_ref[0].astype(F32)
    xs_ref[2, pad:pad + tb, :] = v_ref[0].astype(F32)

    def conv_act(i, cw_ref, h):
        cols = slice(h * hd, (h + 1) * hd)
        w = cw_ref[:, cols]
        base = pad - (taps - 1)
        acc = xs_ref[i, base:base + tb, cols] * w[0:1]
        for j in range(1, taps):
            acc = acc + xs_ref[i, base + j:base + j + tb, cols] * w[j:j + 1]
        return _silu(acc)

    lt = min(LANES, tb)
    nblk = tb // blk

    def slab_of(mat, b):
        c0 = (b * blk) // lt * lt
        return mat[b * blk:(b + 1) * blk, c0:c0 + lt]

    def slabs_of(mat):
        return [slab_of(mat, b) for b in range(nblk)]

    def full_of(slabs):
        rows = []
        for b, sl in enumerate(slabs):
            tile = (b * blk) // lt
            parts = [sl if c == tile else jnp.zeros((blk, lt), sl.dtype) for c in range(tb // lt)]
            rows.append(parts[0] if len(parts) == 1 else jnp.concatenate(parts, axis=1))
        return rows[0] if len(rows) == 1 else jnp.concatenate(rows, axis=0)

    incl, strict, eye = [], [], []
    for b in range(nblk):
        ri = lax.broadcasted_iota(jnp.int32, (blk, lt), 0) + b * blk
        ci = lax.broadcasted_iota(jnp.int32, (blk, lt), 1) + (b * blk) // lt * lt
        same = (ci >= b * blk) & (ci < (b + 1) * blk)
        incl.append(same & (ri >= ci))
        strict.append(same & (ri > ci))
        eye.append((ri == ci).astype(F32))

    ri = lax.broadcasted_iota(jnp.int32, (tb, tb), 0)
    ci = lax.broadcasted_iota(jnp.int32, (tb, tb), 1)
    shift = int(math.log2(blk))
    same = (ri >> shift) == (ci >> shift)
    sum_mats = jnp.concatenate([(same & (ri <= ci)).astype(BF16), same.astype(BF16)], axis=1)
    row_id = lax.broadcasted_iota(jnp.int32, (SUBLANES, tb), 0)

    def split16(x):
        hi = x.astype(BF16)
        return hi, (x - hi.astype(F32)).astype(BF16)

    r = r_ref[0].reshape(hg * SUBLANES, tb)
    dt_rows = jnp.concatenate([jnp.full((SUBLANES, tb), dt_ref[head0 + h], F32) for h in heads], axis=0)
    alog_rows = jnp.concatenate([jnp.full((SUBLANES, tb), alog_ref[head0 + h], F32) for h in heads], axis=0)
    beta_rows = _sigmoid(r)
    xx = r + dt_rows
    softplus = jnp.maximum(xx, 0.0) + jnp.log1p(jnp.exp(-jnp.abs(xx)))
    g_rows = -jnp.exp(alog_rows) * softplus
    g_hi = g_rows.astype(BF16)
    g_rest = g_rows - g_hi.astype(F32)
    g_mid = g_rest.astype(BF16)
    g_lo = (g_rest - g_mid.astype(F32)).astype(BF16)
    sums = (_dot(g_hi, sum_mats) + _dot(g_mid, sum_mats)) + _dot(g_lo, sum_mats)

    q, k, v, neg_m, a_intra, beta_c, g_c, g_last_c = [], [], [], [], [], [], [], []
    for h in heads:
        aq = conv_act(0, cwq_ref, h)
        ak = conv_act(1, cwk_ref, h)
        v.append(conv_act(2, cwv_ref, h))
        q.append(aq * lax.rsqrt(jnp.sum(aq * aq, axis=-1, keepdims=True) + EPS) * (hd ** -0.5))
        k.append(ak * lax.rsqrt(jnp.sum(ak * ak, axis=-1, keepdims=True) + EPS))

        hs = slice(h * SUBLANES, (h + 1) * SUBLANES)
        g_cum_rows = sums[hs, :tb]
        cols = jnp.transpose(jnp.where(row_id == 0, beta_rows[hs], g_cum_rows))
        beta_c.append(cols[:, 0:1])
        g_c.append(cols[:, 1:2])
        g_last_c.append(jnp.transpose(sums[hs, tb:])[:, 1:2])
        g_r = g_cum_rows[1:2, :]

        k16 = k[h].astype(BF16)
        kk = _dot_nt(k16, k16)
        qk = _dot_nt(q[h].astype(BF16), k16)
        neg_m_h, a_h = [], []
        for b in range(nblk):
            rs = slice(b * blk, (b + 1) * blk)
            c0 = (b * blk) // lt * lt
            diff = g_c[h][rs] - g_r[:, c0:c0 + lt]
            decay = jnp.where(incl[b], jnp.exp(jnp.where(incl[b], diff, 0.0)), 0.0)
            neg_m_h.append(jnp.where(strict[b], -(beta_c[h][rs] * slab_of(kk, b)) * decay, 0.0))
            a_h.append((slab_of(qk, b) * decay).astype(BF16))
        neg_m.append(neg_m_h)
        a_intra.append(full_of(a_h))

    for i in range(3):
        xs_ref[i, 0:pad, :] = xs_ref[i, tb:tb + pad, :]

    n_hi, n_lo = [], []
    for h in heads:
        pieces = [split16(m) for m in neg_m[h]]
        n_hi.append(full_of([p[0] for p in pieces]))
        n_lo.append(full_of([p[1] for p in pieces]))
    x = [[eye[b] + neg_m[h][b] for b in range(nblk)] for h in heads]
    steps = shift - 1
    for step in range(steps):
        if step < steps - 1:
            x16 = [full_of([xb.astype(BF16) for xb in x[h]]) for h in heads]
            nx = [_dot(n_hi[h], x16[h]) for h in heads]
        else:
            pieces = [[split16(xb) for xb in x[h]] for h in heads]
            x16 = [full_of([p[0] for p in pieces[h]]) for h in heads]
            x_lo = [full_of([p[1] for p in pieces[h]]) for h in heads]
            nx = [_dot(n_hi[h], x16[h]) + (_dot(n_hi[h], x_lo[h]) + _dot(n_lo[h], x16[h])) for h in heads]
        resid = [full_of([((eye[b] - x[h][b]) + slab_of(nx[h], b)).astype(BF16) for b in range(nblk)])
                 for h in heads]
        upd = [_dot(x16[h], resid[h]) for h in heads]
        x = [[x[h][b] + slab_of(upd[h], b) for b in range(nblk)] for h in heads]
    x = [full_of([xb.astype(BF16) for xb in x[h]]) for h in heads]

    u, w, qg, kg, e_last = [], [], [], [], []
    for h in heads:
        e_g = jnp.exp(g_c[h])
        rhs = jnp.concatenate([v[h] * beta_c[h], k[h] * (beta_c[h] * e_g)], axis=1).astype(BF16)
        uw = _dot(x[h], rhs)
        u.append(uw[:, :hd])
        w.append(uw[:, hd:])
        qg.append(q[h] * e_g)
        kg.append((k[h] * jnp.exp(g_last_c[h] - g_c[h])).astype(BF16))
        e_last.append(jnp.exp(g_last_c[h]))

    s = [s_ref[h] for h in heads]
    v_new = [[] for _ in heads]
    o_inter = [[] for _ in heads]
    for b in range(tb // blk):
        rs = slice(b * blk, (b + 1) * blk)
        for h in heads:
            ws_qs = _dot(jnp.concatenate([w[h][rs], qg[h][rs]], axis=0).astype(BF16), s[h].astype(BF16))
            vn = u[h][rs] - ws_qs[:blk]
            v_new[h].append(vn)
            o_inter[h].append(ws_qs[blk:])
            s[h] = s[h] * e_last[h][b * blk:b * blk + 1, :] + _dot_tn(kg[h][rs], vn.astype(BF16))

    for h in heads:
        s_ref[h] = s[h]
        cols = slice(h * hd, (h + 1) * hd)
        vn = jnp.concatenate(v_new[h], axis=0).astype(BF16)
        o = jnp.concatenate(o_inter[h], axis=0) + _dot(a_intra[h], vn)
        o_ref[0, :, cols] = (_rms(o, nw_ref[...]) * _silu(gate_ref[0, :, cols].astype(F32))).astype(BF16)

    @pl.when(t == pl.num_programs(2) - 1)
    def _():
        sfin_ref[0] = s_ref[...]


def _gdn_call(z, rows, conv_w8, conv_buf8, s0, a_log, dt_bias, norm_w, heads):
    batch, seq, _ = z.shape
    tb = min(seq, GDN_SUPER)
    blk = min(tb, GDN_BLOCK)
    hg = min(heads, GDN_HEADS_PER_STEP)
    assert seq % tb == 0 and tb % blk == 0 and heads % hg == 0
    hd = HEAD_DIM
    wd = hg * hd
    ng = heads // hg
    smem = pl.BlockSpec(memory_space=pltpu.SMEM)

    def zcol(seg):
        return pl.BlockSpec((1, tb, wd), lambda b, g, t: (b, t, seg * ng + g))

    def cw(seg):
        return pl.BlockSpec((SUBLANES, wd), lambda b, g, t: (0, seg * ng + g))

    def cs(seg):
        return pl.BlockSpec((1, SUBLANES, wd), lambda b, g, t: (b, 0, seg * ng + g))

    state_spec = pl.BlockSpec((1, hg, hd, hd), lambda b, g, t: (b, g, 0, 0))
    kern = functools.partial(_gdn_kernel, tb=tb, blk=blk, hg=hg)
    return pl.pallas_call(
        kern,
        grid=(batch, ng, seq // tb),
        in_specs=[smem, smem,
                  zcol(0), zcol(1), zcol(2), zcol(3),
                  pl.BlockSpec((1, hg, SUBLANES, tb), lambda b, g, t: (b, g, 0, t)),
                  cw(0), cw(1), cw(2),
                  cs(0), cs(1), cs(2),
                  state_spec,
                  pl.BlockSpec((1, hd), lambda b, g, t: (0, 0))],
        out_specs=[pl.BlockSpec((1, tb, wd), lambda b, g, t: (b, t, g)),
                   state_spec],
        out_shape=[jax.ShapeDtypeStruct((batch, seq, heads * hd), BF16),
                   jax.ShapeDtypeStruct((batch, heads, hd, hd), F32)],
        scratch_shapes=[pltpu.VMEM((3, tb + SUBLANES, wd), F32),
                        pltpu.VMEM((hg, hd, hd), F32)],
        compiler_params=_params("parallel", "parallel", "arbitrary"),
        name="gated_deltanet",
    )(a_log, dt_bias, z, z, z, z, rows, conv_w8, conv_w8, conv_w8,
      conv_buf8, conv_buf8, conv_buf8, s0, norm_w.reshape(1, hd))


def _attn_prompt_kernel(q_ref, k_ref, v_ref, gate_ref, bias_ref, qw_ref, kw_ref,
                        o_ref, kc_ref, kbuf, vbuf, qs, bias_s, *, tq, hg):
    i = pl.program_id(2)
    hd = HEAD_DIM
    npairs = tq // PAIR_ROWS
    problems = [(h, p) for h in range(hg) for p in range(npairs)]

    for h in range(hg):
        cols = slice(h * hd, (h + 1) * hd)

        @pl.when(i == 0)
        def _():
            kbuf[h, 0:PREV_ROWS, :] = jnp.zeros((PREV_ROWS, hd), BF16)
            vbuf[h, 0:PREV_ROWS, :] = jnp.zeros((PREV_ROWS, hd), BF16)

        @pl.when(i > 0)
        def _():
            kbuf[h, 0:PREV_ROWS, :] = kbuf[h, tq:tq + PREV_ROWS, :]
            vbuf[h, 0:PREV_ROWS, :] = vbuf[h, tq:tq + PREV_ROWS, :]

        kn = _rms(k_ref[0, :, cols].astype(F32), kw_ref[...])
        kc_ref[0, :, cols] = kn
        kbuf[h, PREV_ROWS:PREV_ROWS + tq, :] = kn.astype(BF16)
        vbuf[h, PREV_ROWS:PREV_ROWS + tq, :] = v_ref[0, :, cols]
        qs[h] = (_rms(q_ref[0, :, cols].astype(F32), qw_ref[...]) * (hd ** -0.5)).astype(BF16)

    @pl.when(i == 0)
    def _():
        col = lax.broadcasted_iota(jnp.int32, (PAIR_ROWS, PAIR_COLS), 1)
        for h, p in problems:
            bias_s[h, p] = jnp.where(col < PREV_ROWS - p * PAIR_ROWS, MASKED, bias_ref[h])

    @pl.when(i == 1)
    def _():
        for h, p in problems:
            bias_s[h, p] = bias_ref[h]

    def window(p):
        return slice(p * PAIR_ROWS, p * PAIR_ROWS + PAIR_COLS)

    def rows(p):
        return slice(p * PAIR_ROWS, (p + 1) * PAIR_ROWS)

    s = [_dot_nt(qs[h, rows(p), :], kbuf[h, window(p), :]) + bias_s[h, p] for h, p in problems]
    e = [jnp.exp(x - jnp.max(x, axis=-1, keepdims=True)) for x in s]
    denom = [jnp.sum(x, axis=-1, keepdims=True) for x in e]
    for n, (h, p) in enumerate(problems):
        cols = slice(h * hd, (h + 1) * hd)
        o = _dot(e[n].astype(BF16), vbuf[h, window(p), :]) / denom[n]
        o_ref[0, rows(p), cols] = (o * _silu(gate_ref[0, rows(p), cols].astype(F32))).astype(BF16)


def _attn_prompt_call(z, bias_tab, q_norm_w, k_norm_w, heads):
    batch, seq, _ = z.shape
    tq = PREV_ROWS
    hg = min(heads, ATT_HEADS_PER_STEP)
    assert seq % tq == 0 and tq % PAIR_ROWS == 0 and heads % hg == 0
    hd = HEAD_DIM
    wd = hg * hd
    ng = heads // hg

    def zcol(seg):
        return pl.BlockSpec((1, tq, wd), lambda b, g, i: (b, i, seg * ng + g))

    kern = functools.partial(_attn_prompt_kernel, tq=tq, hg=hg)
    return pl.pallas_call(
        kern,
        grid=(batch, ng, seq // tq),
        in_specs=[zcol(4), zcol(5), zcol(6), zcol(7),
                  pl.BlockSpec((hg, PAIR_ROWS, PAIR_COLS), lambda b, g, i: (g, 0, 0)),
                  pl.BlockSpec((1, hd), lambda b, g, i: (0, 0)),
                  pl.BlockSpec((1, hd), lambda b, g, i: (0, 0))],
        out_specs=[pl.BlockSpec((1, tq, wd), lambda b, g, i: (b, i, g)),
                   pl.BlockSpec((1, tq, wd), lambda b, g, i: (b, 0, g))],
        out_shape=[jax.ShapeDtypeStruct((batch, seq, heads * hd), BF16),
                   jax.ShapeDtypeStruct((batch, tq, heads * hd), F32)],
        scratch_shapes=[pltpu.VMEM((hg, PREV_ROWS + tq, hd), BF16),
                        pltpu.VMEM((hg, PREV_ROWS + tq, hd), BF16),
                        pltpu.VMEM((hg, tq, hd), BF16),
                        pltpu.VMEM((hg, tq // PAIR_ROWS, PAIR_ROWS, PAIR_COLS), F32)],
        compiler_params=_params("parallel", "parallel", "arbitrary"),
        name="band_attention_prompt",
    )(z, z, z, z, bias_tab, q_norm_w.reshape(1, hd), k_norm_w.reshape(1, hd))


def _attn_sample_kernel(q_ref, k_ref, v_ref, gate_ref, ck_ref, cv_ref, bias_ref, qw_ref, kw_ref,
                        o_ref, kn_ref, *, seq, n_cached, heads):
    hd = HEAD_DIM
    for h in range(heads):
        cols = slice(h * hd, (h + 1) * hd)
        kn = _rms(k_ref[0, :, cols].astype(F32), kw_ref[...])
        kn_ref[0, :, cols] = kn
        q16 = (_rms(q_ref[0, :, cols].astype(F32), qw_ref[...]) * (hd ** -0.5)).astype(BF16)
        bias = bias_ref[h]
        s_old = _dot_nt(q16, ck_ref[0, :, cols].astype(BF16)) + bias[:seq, :n_cached]
        s_new = _dot_nt(q16, kn.astype(BF16)) + bias[:seq, n_cached:n_cached + seq]
        m = jnp.maximum(jnp.max(s_old, axis=-1, keepdims=True), jnp.max(s_new, axis=-1, keepdims=True))
        p_old = jnp.exp(s_old - m)
        p_new = jnp.exp(s_new - m)
        denom = jnp.sum(p_old, axis=-1, keepdims=True) + jnp.sum(p_new, axis=-1, keepdims=True)
        o = (_dot(p_old.astype(BF16), cv_ref[0, :, cols].astype(BF16))
             + _dot(p_new.astype(BF16), v_ref[0, :, cols])) / denom
        o_ref[0, :, cols] = (o * _silu(gate_ref[0, :, cols].astype(F32))).astype(BF16)


def _attn_sample_call(z, cache_k, cache_v, bias_tab, q_norm_w, k_norm_w, heads):
    batch, seq, _ = z.shape
    n_cached = cache_k.shape[1]
    hd = HEAD_DIM
    wd = heads * hd
    q_pos = PAST_LEN + np.arange(seq)
    k_pos = PAST_LEN - n_cached + np.arange(n_cached + seq)
    qc, kc = q_pos // CHUNK, k_pos // CHUNK
    valid = (k_pos[None] >= 0) & (kc[None] <= qc[:, None]) & (kc[None] >= qc[:, None] - BAND_PREV)
    assert valid.all() and n_cached == PREV_ROWS and seq <= CHUNK and n_cached + seq <= BAND

    def zcol(seg):
        return pl.BlockSpec((1, seq, wd), lambda b: (b, 0, seg))

    cache_spec = pl.BlockSpec((1, n_cached, wd), lambda b: (b, 0, 0))
    kern = functools.partial(_attn_sample_kernel, seq=seq, n_cached=n_cached, heads=heads)
    return pl.pallas_call(
        kern,
        grid=(batch,),
        in_specs=[zcol(4), zcol(5), zcol(6), zcol(7),
                  cache_spec, cache_spec,
                  pl.BlockSpec((heads, PAIR_ROWS, PAIR_COLS), lambda b: (0, 0, 0)),
                  pl.BlockSpec((1, hd), lambda b: (0, 0)),
                  pl.BlockSpec((1, hd), lambda b: (0, 0))],
        out_specs=[pl.BlockSpec((1, seq, wd), lambda b: (b, 0, 0)),
                   pl.BlockSpec((1, seq, wd), lambda b: (b, 0, 0))],
        out_shape=[jax.ShapeDtypeStruct((batch, seq, wd), BF16),
                   jax.ShapeDtypeStruct((batch, seq, wd), F32)],
        compiler_params=_params("parallel"),
        name="band_attention_sample",
    )(z, z, z, z, cache_k.reshape(batch, n_cached, wd),
      cache_v.reshape(batch, n_cached, wd), bias_tab,
      q_norm_w.reshape(1, hd), k_norm_w.reshape(1, hd))


def _out_kernel(oa_ref, ob_ref, ma_ref, mb_ref, x_ref, g_ref, wa_ref, wb_ref, wo_ref, y_ref):
    bb, tt, d = x_ref.shape
    ya = _dot(oa_ref[...].reshape(bb * tt, -1), wa_ref[...])
    yb = _dot(ob_ref[...].reshape(bb * tt, -1), wb_ref[...])
    merged = (_sigmoid(ma_ref[...].reshape(bb * tt, d).astype(F32)) * ya
              + _sigmoid(mb_ref[...].reshape(bb * tt, d).astype(F32)) * yb)
    delta = _dot(merged.astype(BF16), wo_ref[...]).reshape(bb, tt, d)
    y_ref[...] = x_ref[...] + g_ref[...] * delta


def _out_call(o_a, o_b, z, x, gate, w_proj_a, w_proj_b, w_out, heads):
    batch, seq, d = x.shape
    bb, tt = _row_tiling(batch, seq, 512)
    wa = o_a.shape[-1]
    wb = o_b.shape[-1]
    m_off = (8 * heads * HEAD_DIM) // d
    assert m_off * d == 8 * heads * HEAD_DIM

    def rows(width, off=0):
        return pl.BlockSpec((bb, tt, width), lambda b, i: (b, i, off))

    def whole(shape):
        return pl.BlockSpec(shape, lambda b, i: (0, 0), pipeline_mode=pl.Buffered(1))

    return pl.pallas_call(
        _out_kernel,
        grid=(batch // bb, seq // tt),
        in_specs=[rows(wa), rows(wb), rows(d, m_off), rows(d, m_off + 1), rows(d),
                  pl.BlockSpec((bb, 1, d), lambda b, i: (b, 0, 0)),
                  whole((wa, d)), whole((wb, d)), whole((d, d))],
        out_specs=rows(d),
        out_shape=jax.ShapeDtypeStruct((batch, seq, d), F32),
        compiler_params=_params("parallel", "parallel"),
        name="out_proj",
    )(o_a, o_b, z, z, x, gate, w_proj_a, w_proj_b, w_out)


def _layer(x, mod, conv_buf, s0, cache_k, cache_v, p, bias_tab, heads, prompt):
    (norm_w, w_main, w_small, conv_w8, a_log, dt_bias, gdn_norm_w,
     q_norm_w, k_norm_w, w_proj_a, w_proj_b, w_out) = p
    batch, seq, d = x.shape
    hd = HEAD_DIM
    shift, scale, gate = [m.reshape(batch, 1, d) for m in jnp.split(mod, 3, axis=-1)]

    z, z_small = _inproj_call(x, scale, shift, norm_w, w_main, w_small)

    logits = jnp.transpose(z_small[..., :2 * heads].reshape(batch, seq, 2, heads), (0, 3, 2, 1))
    rows = jnp.pad(logits, ((0, 0), (0, 0), (0, SUBLANES - 2), (0, 0)))
    conv_buf8 = jnp.pad(conv_buf, ((0, 0), (SUBLANES - (CONV_WIDTH - 1), 0), (0, 0)))
    o_a, s_new = _gdn_call(z, rows, conv_w8, conv_buf8, s0, a_log, dt_bias, gdn_norm_w, heads)

    conv_dim = 3 * heads * hd
    new_buf = z[:, seq - (CONV_WIDTH - 1):, :conv_dim].astype(F32)
    if prompt:
        o_b, k_new = _attn_prompt_call(z, bias_tab, q_norm_w, k_norm_w, heads)
        keep = k_new.shape[1]
        v_new = z[:, seq - keep:, 6 * heads * hd:7 * heads * hd]
    else:
        o_b, k_new = _attn_sample_call(z, cache_k, cache_v, bias_tab, q_norm_w, k_norm_w, heads)
        keep = seq
        v_new = z[:, :, 6 * heads * hd:7 * heads * hd]
    k_new = k_new.reshape(batch, keep, heads, hd)
    v_new = v_new.reshape(batch, keep, heads, hd).astype(F32)

    y = _out_call(o_a, o_b, z, x, gate, w_proj_a, w_proj_b, w_out, heads)
    return y, new_buf, s_new, k_new, v_new


def kernel(x_prompt, x_sample, c_prompt, c_sample, state_conv, state_delta, cache_k, cache_v, norm_w, ada_w, ada_b, w_in, conv_w, gdn_a_log, gdn_dt_bias, gdn_norm_w, q_norm_w, k_norm_w, rel_bias, w_proj_a, w_proj_b, w_out):
    bp, _, d = x_prompt.shape
    depth = w_in.shape[0]
    heads = gdn_a_log.shape[1]
    hd = HEAD_DIM
    small0 = 4 * heads * hd
    assert w_in.shape[2] == 8 * heads * hd + 2 * heads + 2 * d
    assert conv_w.shape[1:] == (CONV_WIDTH, 3 * heads * hd) and rel_bias.shape[1:] == (heads, N_REL)

    mod = _mod_call(jnp.concatenate([c_prompt, c_sample], axis=0), ada_w, ada_b)
    bias_tab = _bias_call(rel_bias)

    yp, ys = x_prompt, x_sample
    outs_p, outs_s = [], []
    zero_buf = jnp.zeros((bp, CONV_WIDTH - 1, 3 * heads * hd), x_prompt.dtype)
    zero_state = jnp.zeros((bp, heads, hd, hd), state_delta.dtype)
    for l in range(depth):
        w = w_in[l]
        w_main = jnp.concatenate([w[:, :small0], w[:, small0 + 2 * heads:]], axis=1).astype(BF16)
        w_small = jnp.pad(w[:, small0:small0 + 2 * heads], ((0, 0), (0, LANES - 2 * heads))).astype(BF16)
        conv_w8 = jnp.pad(conv_w[l], ((0, SUBLANES - CONV_WIDTH), (0, 0)))
        p = (norm_w[l], w_main, w_small, conv_w8, gdn_a_log[l], gdn_dt_bias[l], gdn_norm_w[l],
             q_norm_w[l], k_norm_w[l], w_proj_a[l].astype(BF16), w_proj_b[l].astype(BF16),
             w_out[l].astype(BF16))
        yp, *rest = _layer(yp, mod[l, :bp], zero_buf, zero_state, None, None, p, bias_tab[l], heads, True)
        outs_p.append(rest)
        ys, *rest = _layer(ys, mod[l, bp:], state_conv[l], state_delta[l], cache_k[l], cache_v[l],
                           p, bias_tab[l], heads, False)
        outs_s.append(rest)

    stack = lambda outs, i: jnp.stack([o[i] for o in outs])
    return (yp, ys,
            stack(outs_p, 0), stack(outs_p, 1), stack(outs_p, 2), stack(outs_p, 3),
            stack(outs_s, 0), stack(outs_s, 1), stack(outs_s, 2), stack(outs_s, 3))
```

```python
import functools
import math

import numpy as np
import jax
import jax.numpy as jnp
from jax import lax
from jax.experimental import pallas as pl
from jax.experimental.pallas import tpu as pltpu

F32 = jnp.float32
BF16 = jnp.bfloat16
HIGHEST = lax.Precision.HIGHEST

EPS = 1e-6
CHUNK = 64
BAND_PREV = 8
REL_CLIP = 128
N_REL = 2 * REL_CLIP + 1
PAST_LEN = 2048
CONV_WIDTH = 4
HEAD_DIM = 128
LANES = 128
SUBLANES = 8
VMEM_LIMIT_BYTES = 56 * 1024 * 1024
GDN_BLOCK = 64
GDN_SUPER = 256
GDN_HEADS_PER_STEP = 8
PREV_ROWS = BAND_PREV * CHUNK
BAND = PREV_ROWS + CHUNK
PAIR_ROWS = 2 * CHUNK
PAIR_COLS = PREV_ROWS + PAIR_ROWS
MASKED = -1e30
UNIT_Q_A, UNIT_GATE_A, UNIT_K_A, UNIT_Q_B, UNIT_V_A, UNIT_K_B, UNIT_V_B, UNIT_GATE_B = range(8)
CONV_UNITS = {UNIT_Q_A: "q", UNIT_K_A: "k", UNIT_V_A: "v"}
N_HEAD_UNITS = 8
ATT_HEADS_PER_STEP = 2


def _dot(a, b):
    return jnp.dot(a, b, preferred_element_type=F32)


def _dot_nt(a, b):
    return lax.dot_general(a, b, (((1,), (1,)), ((), ())), preferred_element_type=F32)


def _dot_tn(a, b):
    return lax.dot_general(a, b, (((0,), (0,)), ((), ())), preferred_element_type=F32)


def _sigmoid(x):
    return 0.5 + 0.5 * jnp.tanh(0.5 * x)


def _silu(x):
    half = 0.5 * x
    return half + half * jnp.tanh(half)


def _rms(x, w):
    return x * lax.rsqrt(jnp.mean(x * x, axis=-1, keepdims=True) + EPS) * w


def _params(*sem):
    return pltpu.CompilerParams(dimension_semantics=sem, vmem_limit_bytes=VMEM_LIMIT_BYTES)


def _largest_divisor(n, candidates):
    for c in candidates:
        if n % c == 0:
            return c
    raise ValueError(f"no tile in {candidates} divides {n}")


def _row_tiling(batch, seq, rows):
    if seq >= rows:
        assert seq % rows == 0
        return 1, rows
    bb = min(batch, rows // seq)
    assert batch % bb == 0
    return bb, seq


def _mod_kernel(c_ref, w_ref, b_ref, o_ref):
    o_ref[0] = _dot(_silu(c_ref[...]).astype(BF16), w_ref[0].astype(BF16)) + b_ref[0]


def _mod_call(c, ada_w, ada_b):
    depth, d, n = ada_w.shape
    rows = c.shape[0]
    tn = _largest_divisor(n, (1024, 512, 256, 128))
    return pl.pallas_call(
        _mod_kernel,
        grid=(depth, n // tn),
        in_specs=[pl.BlockSpec((rows, d), lambda l, j: (0, 0)),
                  pl.BlockSpec((1, d, tn), lambda l, j: (l, 0, j)),
                  pl.BlockSpec((1, 1, tn), lambda l, j: (l, 0, j))],
        out_specs=pl.BlockSpec((1, rows, tn), lambda l, j: (l, 0, j)),
        out_shape=jax.ShapeDtypeStruct((depth, rows, n), F32),
        compiler_params=_params("parallel", "parallel"),
        name="adaln_mod",
    )(c, ada_w, ada_b.reshape(depth, 1, n))


def _bias_kernel(rb_ref, o_ref):
    rb = rb_ref[0]
    heads, nrel = rb.shape
    rb_hi = rb.astype(BF16)
    rest = rb - rb_hi.astype(F32)
    rb_mid = rest.astype(BF16)
    rb_lo = (rest - rb_mid.astype(F32)).astype(BF16)
    r = lax.broadcasted_iota(jnp.int32, (nrel, PAIR_COLS), 0)
    j = lax.broadcasted_iota(jnp.int32, (nrel, PAIR_COLS), 1)
    key_chunk = lax.broadcasted_iota(jnp.int32, (heads, PAIR_COLS), 1) // CHUNK

    def body(i, carry):
        dist = jnp.clip(PREV_ROWS + i - j, -REL_CLIP, REL_CLIP) + REL_CLIP
        onehot = (dist == r).astype(BF16)
        row = (_dot(rb_hi, onehot) + _dot(rb_mid, onehot)) + _dot(rb_lo, onehot)
        q_chunk = i // CHUNK
        in_band = (key_chunk >= q_chunk) & (key_chunk <= q_chunk + BAND_PREV)
        o_ref[0, i] = jnp.where(in_band, row, MASKED)
        return carry

    lax.fori_loop(0, PAIR_ROWS, body, 0)


def _bias_call(rel_bias):
    depth, heads, nrel = rel_bias.shape
    nrel_pad = -(-nrel // SUBLANES) * SUBLANES
    rb = jnp.pad(rel_bias, ((0, 0), (0, 0), (0, nrel_pad - nrel)))
    tab = pl.pallas_call(
        _bias_kernel,
        grid=(depth,),
        in_specs=[pl.BlockSpec((1, heads, nrel_pad), lambda l: (l, 0, 0))],
        out_specs=pl.BlockSpec((1, PAIR_ROWS, heads, PAIR_COLS), lambda l: (l, 0, 0, 0)),
        out_shape=jax.ShapeDtypeStruct((depth, PAIR_ROWS, heads, PAIR_COLS), F32),
        compiler_params=_params("parallel"),
        name="rel_bias_table",
    )(rb)
    return jnp.transpose(tab, (0, 2, 1, 3))


def _inproj_kernel(x_ref, sc_ref, sh_ref, nw_ref, w_ref, ws_ref, cw_ref, ci_ref,
                   z_ref, zs_ref, tail_ref, h_ref, cbuf_ref, carry_ref,
                   *, bb, tt, rc, tn, n_conv_tiles, heads):
    i = pl.program_id(1)
    j = pl.program_id(2)
    hd = HEAD_DIM
    pad = SUBLANES
    taps = CONV_WIDTH
    rows = bb * tt
    piece = min(tt, rc)

    @pl.when(j == 0)
    def _():
        for b in range(bb):
            for r0 in range(0, tt, piece):
                x = x_ref[b, r0:r0 + piece, :]
                h = _rms(x, nw_ref[...]) * (1.0 + sc_ref[b]) + sh_ref[b]
                h_ref[b * tt + r0:b * tt + r0 + piece, :] = h.astype(BF16)
        zs_ref[...] = _dot(h_ref[...], ws_ref[...]).reshape(zs_ref.shape)

    def store(r0, n, cols, val):
        b, t0 = divmod(r0, tt)
        z_ref[b, t0:t0 + n, cols] = val.astype(z_ref.dtype)

    def plain_tile():
        for r0 in range(0, rows, rc):
            acc = _dot(h_ref[r0:r0 + rc, :], w_ref[...])
            for p0 in range(0, rc, piece):
                store(r0 + p0, piece, slice(None), acc[p0:p0 + piece])

    def conv_tile(jt):
        @pl.when(i == 0)
        def _():
            carry_ref[jt] = ci_ref[...]

        base = pad - (taps - 1)
        for r0 in range(0, rows, rc):
            acc = _dot(h_ref[r0:r0 + rc, :], w_ref[...])
            for p0 in range(0, rc, piece):
                b, t0 = divmod(r0 + p0, tt)
                if t0 == 0:
                    cbuf_ref[0:pad, :] = carry_ref[jt, b]
                cbuf_ref[pad:pad + piece, :] = acc[p0:p0 + piece]
                for g in range(tn // hd):
                    cols = slice(g * hd, (g + 1) * hd)
                    kind = CONV_UNITS.get((jt * tn + g * hd) // (heads * hd))
                    if kind is None:
                        store(r0 + p0, piece, cols, acc[p0:p0 + piece, cols])
                        continue
                    cw = cw_ref[:, cols]
                    conv = cbuf_ref[base:base + piece, cols] * cw[0:1]
                    for n in range(1, taps):
                        conv = conv + cbuf_ref[base + n:base + n + piece, cols] * cw[n:n + 1]
                    act = _silu(conv)
                    if kind in ("q", "k"):
                        act = act * lax.rsqrt(jnp.sum(act * act, axis=-1, keepdims=True) + EPS)
                    if kind == "q":
                        act = act * (hd ** -0.5)
                    store(r0 + p0, piece, cols, act)
                cbuf_ref[0:pad, :] = cbuf_ref[piece:piece + pad, :]
                if t0 + piece == tt:
                    carry_ref[jt, b] = cbuf_ref[0:pad, :]
                    tail_ref[b] = cbuf_ref[0:pad, :]

    for jt in range(n_conv_tiles):
        pl.when(j == jt)(functools.partial(conv_tile, jt))
    pl.when(j >= n_conv_tiles)(plain_tile)


def _inproj_call(x, scale, shift, norm_w, w_main, w_small, conv_w, conv_buf, heads):
    batch, seq, d = x.shape
    n = w_main.shape[1]
    ns = w_small.shape[1]
    bb, tt = _row_tiling(batch, seq, 1024)
    tn = _largest_divisor(n, (2048, 1024, 512, 256, 128))
    unit = heads * HEAD_DIM
    n_conv_tiles = -(-(max(CONV_UNITS) + 1) * unit // tn)
    conv_cols = n_conv_tiles * tn

    def by_unit(a):
        out = jnp.zeros(a.shape[:-1] + (conv_cols,), a.dtype)
        for n, u in enumerate(sorted(CONV_UNITS)):
            out = lax.dynamic_update_slice_in_dim(out, a[..., n * unit:(n + 1) * unit], u * unit, axis=-1)
        return out

    conv_w8 = by_unit(jnp.pad(conv_w, ((0, SUBLANES - CONV_WIDTH), (0, 0))))
    conv_buf8 = by_unit(jnp.pad(conv_buf, ((0, 0), (SUBLANES - (CONV_WIDTH - 1), 0), (0, 0))))
    rc = min(bb * tt, 256)
    kern = functools.partial(_inproj_kernel, bb=bb, tt=tt, rc=rc, tn=tn,
                             n_conv_tiles=n_conv_tiles, heads=heads)
    last_conv = n_conv_tiles - 1
    z, zs, tail = pl.pallas_call(
        kern,
        grid=(batch // bb, seq // tt, n // tn),
        in_specs=[pl.BlockSpec((bb, tt, d), lambda b, i, j: (b, i, 0)),
                  pl.BlockSpec((bb, 1, d), lambda b, i, j: (b, 0, 0)),
                  pl.BlockSpec((bb, 1, d), lambda b, i, j: (b, 0, 0)),
                  pl.BlockSpec((1, d), lambda b, i, j: (0, 0)),
                  pl.BlockSpec((d, tn), lambda b, i, j: (0, j)),
                  pl.BlockSpec((d, ns), lambda b, i, j: (0, 0)),
                  pl.BlockSpec((SUBLANES, tn), lambda b, i, j: (0, jnp.minimum(j, last_conv))),
                  pl.BlockSpec((bb, SUBLANES, tn), lambda b, i, j: (b, 0, jnp.minimum(j, last_conv)))],
        out_specs=[pl.BlockSpec((bb, tt, tn), lambda b, i, j: (b, i, j)),
                   pl.BlockSpec((bb, tt, ns), lambda b, i, j: (b, i, 0)),
                   pl.BlockSpec((bb, SUBLANES, tn), lambda b, i, j: (b, 0, jnp.minimum(j, last_conv)))],
        out_shape=[jax.ShapeDtypeStruct((batch, seq, n), BF16),
                   jax.ShapeDtypeStruct((batch, seq, ns), F32),
                   jax.ShapeDtypeStruct((batch, SUBLANES, conv_cols), F32)],
        scratch_shapes=[pltpu.VMEM((bb * tt, d), BF16),
                        pltpu.VMEM((min(tt, rc) + SUBLANES, tn), F32),
                        pltpu.VMEM((n_conv_tiles, bb, SUBLANES, tn), F32)],
        compiler_params=_params("parallel", "arbitrary", "arbitrary"),
        name="in_proj",
    )(x, scale, shift, norm_w.reshape(1, d), w_main, w_small, conv_w8, conv_buf8)
    tail = tail[:, SUBLANES - (CONV_WIDTH - 1):, :]
    new_buf = jnp.concatenate([tail[..., u * unit:(u + 1) * unit] for u in sorted(CONV_UNITS)], axis=-1)
    return z, zs, new_buf


def _gdn_kernel(alog_ref, dt_ref, q_ref, k_ref, v_ref, gate_ref, r_ref, s0_ref, nw_ref,
                o_ref, sfin_ref, s_ref, *, tb, blk, hg):
    head0 = pl.program_id(1) * hg
    t = pl.program_id(2)
    hd = HEAD_DIM
    heads = range(hg)

    @pl.when(t == 0)
    def _():
        s_ref[...] = s0_ref[0]

    lt = min(LANES, tb)
    nblk = tb // blk

    def slab_of(mat, b):
        c0 = (b * blk) // lt * lt
        return mat[b * blk:(b + 1) * blk, c0:c0 + lt]

    def slabs_of(mat):
        return [slab_of(mat, b) for b in range(nblk)]

    def full_of(slabs):
        rows = []
        for b, sl in enumerate(slabs):
            tile = (b * blk) // lt
            parts = [sl if c == tile else jnp.zeros((blk, lt), sl.dtype) for c in range(tb // lt)]
            rows.append(parts[0] if len(parts) == 1 else jnp.concatenate(parts, axis=1))
        return rows[0] if len(rows) == 1 else jnp.concatenate(rows, axis=0)

    incl, strict, eye = [], [], []
    for b in range(nblk):
        ri = lax.broadcasted_iota(jnp.int32, (blk, lt), 0) + b * blk
        ci = lax.broadcasted_iota(jnp.int32, (blk, lt), 1) + (b * blk) // lt * lt
        same = (ci >= b * blk) & (ci < (b + 1) * blk)
        incl.append(same & (ri >= ci))
        strict.append(same & (ri > ci))
        eye.append((ri == ci).astype(F32))

    ri = lax.broadcasted_iota(jnp.int32, (tb, tb), 0)
    ci = lax.broadcasted_iota(jnp.int32, (tb, tb), 1)
    shift = int(math.log2(blk))
    same = (ri >> shift) == (ci >> shift)
    sum_mats = jnp.concatenate([(same & (ri <= ci)).astype(BF16), same.astype(BF16)], axis=1)
    row_id = lax.broadcasted_iota(jnp.int32, (SUBLANES, tb), 0)

    def split16(x):
        hi = x.astype(BF16)
        return hi, (x - hi.astype(F32)).astype(BF16)

    r = r_ref[0].reshape(hg * SUBLANES, tb)
    dt_rows = jnp.concatenate([jnp.full((SUBLANES, tb), dt_ref[head0 + h], F32) for h in heads], axis=0)
    alog_rows = jnp.concatenate([jnp.full((SUBLANES, tb), alog_ref[head0 + h], F32) for h in heads], axis=0)
    beta_rows = _sigmoid(r)
    xx = r + dt_rows
    softplus = jnp.maximum(xx, 0.0) + jnp.log1p(jnp.exp(-jnp.abs(xx)))
    g_rows = -jnp.exp(alog_rows) * softplus
    g_hi = g_rows.astype(BF16)
    g_rest = g_rows - g_hi.astype(F32)
    g_mid = g_rest.astype(BF16)
    g_lo = (g_rest - g_mid.astype(F32)).astype(BF16)
    sums = (_dot(g_hi, sum_mats) + _dot(g_mid, sum_mats)) + _dot(g_lo, sum_mats)

    q, k, v, neg_m, a_intra, beta_c, g_c, g_last_c = [], [], [], [], [], [], [], []
    for h in heads:
        hcols = slice(h * hd, (h + 1) * hd)
        q.append(q_ref[0, :, hcols].astype(F32))
        k.append(k_ref[0, :, hcols].astype(F32))
        v.append(v_ref[0, :, hcols].astype(F32))

        hs = slice(h * SUBLANES, (h + 1) * SUBLANES)
        g_cum_rows = sums[hs, :tb]
        cols = jnp.transpose(jnp.where(row_id == 0, beta_rows[hs], g_cum_rows))
        beta_c.append(cols[:, 0:1])
        g_c.append(cols[:, 1:2])
        g_last_c.append(jnp.transpose(sums[hs, tb:])[:, 1:2])
        g_r = g_cum_rows[1:2, :]

        k16 = k_ref[0, :, hcols]
        kk = _dot_nt(k16, k16)
        qk = _dot_nt(q_ref[0, :, hcols], k16)
        neg_m_h, a_h = [], []
        for b in range(nblk):
            rs = slice(b * blk, (b + 1) * blk)
            c0 = (b * blk) // lt * lt
            diff = g_c[h][rs] - g_r[:, c0:c0 + lt]
            decay = jnp.where(incl[b], jnp.exp(jnp.where(incl[b], diff, 0.0)), 0.0)
            neg_m_h.append(jnp.where(strict[b], -(beta_c[h][rs] * slab_of(kk, b)) * decay, 0.0))
            a_h.append((slab_of(qk, b) * decay).astype(BF16))
        neg_m.append(neg_m_h)
        a_intra.append(full_of(a_h))

    n_hi, n_lo = [], []
    for h in heads:
        pieces = [split16(m) for m in neg_m[h]]
        n_hi.append(full_of([p[0] for p in pieces]))
        n_lo.append(full_of([p[1] for p in pieces]))
    x = [[eye[b] + neg_m[h][b] for b in range(nblk)] for h in heads]
    steps = shift - 1
    for step in range(steps):
        if step < steps - 1:
            x16 = [full_of([xb.astype(BF16) for xb in x[h]]) for h in heads]
            nx = [_dot(n_hi[h], x16[h]) for h in heads]
        else:
            pieces = [[split16(xb) for xb in x[h]] for h in heads]
            x16 = [full_of([p[0] for p in pieces[h]]) for h in heads]
            x_lo = [full_of([p[1] for p in pieces[h]]) for h in heads]
            nx = [_dot(n_hi[h], x16[h]) + (_dot(n_hi[h], x_lo[h]) + _dot(n_lo[h], x16[h])) for h in heads]
        resid = [full_of([((eye[b] - x[h][b]) + slab_of(nx[h], b)).astype(BF16) for b in range(nblk)])
                 for h in heads]
        upd = [_dot(x16[h], resid[h]) for h in heads]
        x = [[x[h][b] + slab_of(upd[h], b) for b in range(nblk)] for h in heads]
    x = [full_of([xb.astype(BF16) for xb in x[h]]) for h in heads]

    u, w, qg, kg, e_last = [], [], [], [], []
    for h in heads:
        e_g = jnp.exp(g_c[h])
        rhs = jnp.concatenate([v[h] * beta_c[h], k[h] * (beta_c[h] * e_g)], axis=1).astype(BF16)
        uw = _dot(x[h], rhs)
        u.append(uw[:, :hd])
        w.append(uw[:, hd:])
        qg.append(q[h] * e_g)
        kg.append((k[h] * jnp.exp(g_last_c[h] - g_c[h])).astype(BF16))
        e_last.append(jnp.exp(g_last_c[h]))

    s = [s_ref[h] for h in heads]
    v_new = [[] for _ in heads]
    o_inter = [[] for _ in heads]
    for b in range(tb // blk):
        rs = slice(b * blk, (b + 1) * blk)
        for h in heads:
            ws_qs = _dot(jnp.concatenate([w[h][rs], qg[h][rs]], axis=0).astype(BF16), s[h].astype(BF16))
            vn = u[h][rs] - ws_qs[:blk]
            v_new[h].append(vn)
            o_inter[h].append(ws_qs[blk:])
            s[h] = s[h] * e_last[h][b * blk:b * blk + 1, :] + _dot_tn(kg[h][rs], vn.astype(BF16))

    for h in heads:
        s_ref[h] = s[h]
        cols = slice(h * hd, (h + 1) * hd)
        vn = jnp.concatenate(v_new[h], axis=0).astype(BF16)
        o = jnp.concatenate(o_inter[h], axis=0) + _dot(a_intra[h], vn)
        o_ref[0, :, cols] = (_rms(o, nw_ref[...]) * _silu(gate_ref[0, :, cols].astype(F32))).astype(BF16)

    @pl.when(t == pl.num_programs(2) - 1)
    def _():
        sfin_ref[0] = s_ref[...]


def _gdn_call(z, rows, s0, a_log, dt_bias, norm_w, heads):
    batch, seq, _ = z.shape
    tb = min(seq, GDN_SUPER)
    blk = min(tb, GDN_BLOCK)
    hg = min(heads, GDN_HEADS_PER_STEP)
    assert seq % tb == 0 and tb % blk == 0 and heads % hg == 0
    hd = HEAD_DIM
    wd = hg * hd
    ng = heads // hg
    smem = pl.BlockSpec(memory_space=pltpu.SMEM)

    def zcol(seg):
        return pl.BlockSpec((1, tb, wd), lambda b, g, t: (b, t, seg * ng + g))

    state_spec = pl.BlockSpec((1, hg, hd, hd), lambda b, g, t: (b, g, 0, 0))
    kern = functools.partial(_gdn_kernel, tb=tb, blk=blk, hg=hg)
    return pl.pallas_call(
        kern,
        grid=(batch, ng, seq // tb),
        in_specs=[smem, smem,
                  zcol(UNIT_Q_A), zcol(UNIT_K_A), zcol(UNIT_V_A), zcol(UNIT_GATE_A),
                  pl.BlockSpec((1, hg, SUBLANES, tb), lambda b, g, t: (b, g, 0, t)),
                  state_spec,
                  pl.BlockSpec((1, hd), lambda b, g, t: (0, 0))],
        out_specs=[pl.BlockSpec((1, tb, wd), lambda b, g, t: (b, t, g)),
                   state_spec],
        out_shape=[jax.ShapeDtypeStruct((batch, seq, heads * hd), BF16),
                   jax.ShapeDtypeStruct((batch, heads, hd, hd), F32)],
        scratch_shapes=[pltpu.VMEM((hg, hd, hd), F32)],
        compiler_params=_params("parallel", "parallel", "arbitrary"),
        name="gated_deltanet",
    )(a_log, dt_bias, z, z, z, z, rows, s0, norm_w.reshape(1, hd))


def _attn_prompt_kernel(q_ref, k_ref, v_ref, gate_ref, bias_ref, qw_ref, kw_ref,
                        o_ref, kc_ref, kbuf, vbuf, qs, bias_s, *, tq, hg):
    i = pl.program_id(2)
    hd = HEAD_DIM
    npairs = tq // PAIR_ROWS
    problems = [(h, p) for h in range(hg) for p in range(npairs)]

    for h in range(hg):
        cols = slice(h * hd, (h + 1) * hd)

        @pl.when(i == 0)
        def _():
            kbuf[h, 0:PREV_ROWS, :] = jnp.zeros((PREV_ROWS, hd), BF16)
            vbuf[h, 0:PREV_ROWS, :] = jnp.zeros((PREV_ROWS, hd), BF16)

        @pl.when(i > 0)
        def _():
            kbuf[h, 0:PREV_ROWS, :] = kbuf[h, tq:tq + PREV_ROWS, :]
            vbuf[h, 0:PREV_ROWS, :] = vbuf[h, tq:tq + PREV_ROWS, :]

        kn = _rms(k_ref[0, :, cols].astype(F32), kw_ref[...])
        kc_ref[0, :, cols] = kn
        kbuf[h, PREV_ROWS:PREV_ROWS + tq, :] = kn.astype(BF16)
        vbuf[h, PREV_ROWS:PREV_ROWS + tq, :] = v_ref[0, :, cols]
        qs[h] = (_rms(q_ref[0, :, cols].astype(F32), qw_ref[...]) * (hd ** -0.5)).astype(BF16)

    @pl.when(i == 0)
    def _():
        col = lax.broadcasted_iota(jnp.int32, (PAIR_ROWS, PAIR_COLS), 1)
        for h, p in problems:
            bias_s[h, p] = jnp.where(col < PREV_ROWS - p * PAIR_ROWS, MASKED, bias_ref[h])

    @pl.when(i == 1)
    def _():
        for h, p in problems:
            bias_s[h, p] = bias_ref[h]

    def window(p):
        return slice(p * PAIR_ROWS, p * PAIR_ROWS + PAIR_COLS)

    def rows(p):
        return slice(p * PAIR_ROWS, (p + 1) * PAIR_ROWS)

    s = [_dot_nt(qs[h, rows(p), :], kbuf[h, window(p), :]) + bias_s[h, p] for h, p in problems]
    e = [jnp.exp(x - jnp.max(x, axis=-1, keepdims=True)) for x in s]
    denom = [jnp.sum(x, axis=-1, keepdims=True) for x in e]
    for n, (h, p) in enumerate(problems):
        cols = slice(h * hd, (h + 1) * hd)
        o = _dot(e[n].astype(BF16), vbuf[h, window(p), :]) / denom[n]
        o_ref[0, rows(p), cols] = (o * _silu(gate_ref[0, rows(p), cols].astype(F32))).astype(BF16)


def _attn_prompt_call(z, bias_tab, q_norm_w, k_norm_w, heads):
    batch, seq, _ = z.shape
    tq = PREV_ROWS
    hg = min(heads, ATT_HEADS_PER_STEP)
    assert seq % tq == 0 and tq % PAIR_ROWS == 0 and heads % hg == 0
    hd = HEAD_DIM
    wd = hg * hd
    ng = heads // hg

    def zcol(seg):
        return pl.BlockSpec((1, tq, wd), lambda b, g, i: (b, i, seg * ng + g))

    kern = functools.partial(_attn_prompt_kernel, tq=tq, hg=hg)
    return pl.pallas_call(
        kern,
        grid=(batch, ng, seq // tq),
        in_specs=[zcol(UNIT_Q_B), zcol(UNIT_K_B), zcol(UNIT_V_B), zcol(UNIT_GATE_B),
                  pl.BlockSpec((hg, PAIR_ROWS, PAIR_COLS), lambda b, g, i: (g, 0, 0)),
                  pl.BlockSpec((1, hd), lambda b, g, i: (0, 0)),
                  pl.BlockSpec((1, hd), lambda b, g, i: (0, 0))],
        out_specs=[pl.BlockSpec((1, tq, wd), lambda b, g, i: (b, i, g)),
                   pl.BlockSpec((1, tq, wd), lambda b, g, i: (b, 0, g))],
        out_shape=[jax.ShapeDtypeStruct((batch, seq, heads * hd), BF16),
                   jax.ShapeDtypeStruct((batch, tq, heads * hd), F32)],
        scratch_shapes=[pltpu.VMEM((hg, PREV_ROWS + tq, hd), BF16),
                        pltpu.VMEM((hg, PREV_ROWS + tq, hd), BF16),
                        pltpu.VMEM((hg, tq, hd), BF16),
                        pltpu.VMEM((hg, tq // PAIR_ROWS, PAIR_ROWS, PAIR_COLS), F32)],
        compiler_params=_params("parallel", "parallel", "arbitrary"),
        name="band_attention_prompt",
    )(z, z, z, z, bias_tab, q_norm_w.reshape(1, hd), k_norm_w.reshape(1, hd))


def _attn_sample_kernel(q_ref, k_ref, v_ref, gate_ref, ck_ref, cv_ref, bias_ref, qw_ref, kw_ref,
                        o_ref, kn_ref, *, seq, n_cached, heads):
    hd = HEAD_DIM
    for h in range(heads):
        cols = slice(h * hd, (h + 1) * hd)
        kn = _rms(k_ref[0, :, cols].astype(F32), kw_ref[...])
        kn_ref[0, :, cols] = kn
        q16 = (_rms(q_ref[0, :, cols].astype(F32), qw_ref[...]) * (hd ** -0.5)).astype(BF16)
        bias = bias_ref[h]
        s_old = _dot_nt(q16, ck_ref[0, :, cols].astype(BF16)) + bias[:seq, :n_cached]
        s_new = _dot_nt(q16, kn.astype(BF16)) + bias[:seq, n_cached:n_cached + seq]
        m = jnp.maximum(jnp.max(s_old, axis=-1, keepdims=True), jnp.max(s_new, axis=-1, keepdims=True))
        p_old = jnp.exp(s_old - m)
        p_new = jnp.exp(s_new - m)
        denom = jnp.sum(p_old, axis=-1, keepdims=True) + jnp.sum(p_new, axis=-1, keepdims=True)
        o = (_dot(p_old.astype(BF16), cv_ref[0, :, cols].astype(BF16))
             + _dot(p_new.astype(BF16), v_ref[0, :, cols])) / denom
        o_ref[0, :, cols] = (o * _silu(gate_ref[0, :, cols].astype(F32))).astype(BF16)


def _attn_sample_call(z, cache_k, cache_v, bias_tab, q_norm_w, k_norm_w, heads):
    batch, seq, _ = z.shape
    n_cached = cache_k.shape[1]
    hd = HEAD_DIM
    wd = heads * hd
    q_pos = PAST_LEN + np.arange(seq)
    k_pos = PAST_LEN - n_cached + np.arange(n_cached + seq)
    qc, kc = q_pos // CHUNK, k_pos // CHUNK
    valid = (k_pos[None] >= 0) & (kc[None] <= qc[:, None]) & (kc[None] >= qc[:, None] - BAND_PREV)
    assert valid.all() and n_cached == PREV_ROWS and seq <= CHUNK and n_cached + seq <= BAND

    def zcol(seg):
        return pl.BlockSpec((1, seq, wd), lambda b: (b, 0, seg))

    cache_spec = pl.BlockSpec((1, n_cached, wd), lambda b: (b, 0, 0))
    kern = functools.partial(_attn_sample_kernel, seq=seq, n_cached=n_cached, heads=heads)
    return pl.pallas_call(
        kern,
        grid=(batch,),
        in_specs=[zcol(UNIT_Q_B), zcol(UNIT_K_B), zcol(UNIT_V_B), zcol(UNIT_GATE_B),
                  cache_spec, cache_spec,
                  pl.BlockSpec((heads, PAIR_ROWS, PAIR_COLS), lambda b: (0, 0, 0)),
                  pl.BlockSpec((1, hd), lambda b: (0, 0)),
                  pl.BlockSpec((1, hd), lambda b: (0, 0))],
        out_specs=[pl.BlockSpec((1, seq, wd), lambda b: (b, 0, 0)),
                   pl.BlockSpec((1, seq, wd), lambda b: (b, 0, 0))],
        out_shape=[jax.ShapeDtypeStruct((batch, seq, wd), BF16),
                   jax.ShapeDtypeStruct((batch, seq, wd), F32)],
        compiler_params=_params("parallel"),
        name="band_attention_sample",
    )(z, z, z, z, cache_k.reshape(batch, n_cached, wd),
      cache_v.reshape(batch, n_cached, wd), bias_tab,
      q_norm_w.reshape(1, hd), k_norm_w.reshape(1, hd))


def _out_kernel(oa_ref, ob_ref, ma_ref, mb_ref, x_ref, g_ref, wa_ref, wb_ref, wo_ref, y_ref):
    bb, tt, d = x_ref.shape
    ya = _dot(oa_ref[...].reshape(bb * tt, -1), wa_ref[...])
    yb = _dot(ob_ref[...].reshape(bb * tt, -1), wb_ref[...])
    merged = (_sigmoid(ma_ref[...].reshape(bb * tt, d).astype(F32)) * ya
              + _sigmoid(mb_ref[...].reshape(bb * tt, d).astype(F32)) * yb)
    delta = _dot(merged.astype(BF16), wo_ref[...]).reshape(bb, tt, d)
    y_ref[...] = x_ref[...] + g_ref[...] * delta


def _out_call(o_a, o_b, z, x, gate, w_proj_a, w_proj_b, w_out, heads):
    batch, seq, d = x.shape
    bb, tt = _row_tiling(batch, seq, 512)
    wa = o_a.shape[-1]
    wb = o_b.shape[-1]
    m_off = (N_HEAD_UNITS * heads * HEAD_DIM) // d
    assert m_off * d == N_HEAD_UNITS * heads * HEAD_DIM

    def rows(width, off=0):
        return pl.BlockSpec((bb, tt, width), lambda b, i: (b, i, off))

    def whole(shape):
        return pl.BlockSpec(shape, lambda b, i: (0, 0), pipeline_mode=pl.Buffered(1))

    return pl.pallas_call(
        _out_kernel,
        grid=(batch // bb, seq // tt),
        in_specs=[rows(wa), rows(wb), rows(d, m_off), rows(d, m_off + 1), rows(d),
                  pl.BlockSpec((bb, 1, d), lambda b, i: (b, 0, 0)),
                  whole((wa, d)), whole((wb, d)), whole((d, d))],
        out_specs=rows(d),
        out_shape=jax.ShapeDtypeStruct((batch, seq, d), F32),
        compiler_params=_params("parallel", "parallel"),
        name="out_proj",
    )(o_a, o_b, z, z, x, gate, w_proj_a, w_proj_b, w_out)


def _layer(x, mod, conv_buf, s0, cache_k, cache_v, p, bias_tab, heads, prompt):
    (norm_w, w_main, w_small, conv_w, a_log, dt_bias, gdn_norm_w,
     q_norm_w, k_norm_w, w_proj_a, w_proj_b, w_out) = p
    batch, seq, d = x.shape
    hd = HEAD_DIM
    shift, scale, gate = [m.reshape(batch, 1, d) for m in jnp.split(mod, 3, axis=-1)]

    z, z_small, new_buf = _inproj_call(x, scale, shift, norm_w, w_main, w_small, conv_w, conv_buf, heads)

    logits = jnp.transpose(z_small[..., :2 * heads].reshape(batch, seq, 2, heads), (0, 3, 2, 1))
    rows = jnp.pad(logits, ((0, 0), (0, 0), (0, SUBLANES - 2), (0, 0)))
    o_a, s_new = _gdn_call(z, rows, s0, a_log, dt_bias, gdn_norm_w, heads)

    if prompt:
        o_b, k_new = _attn_prompt_call(z, bias_tab, q_norm_w, k_norm_w, heads)
        keep = k_new.shape[1]
        v_new = z[:, seq - keep:, UNIT_V_B * heads * hd:(UNIT_V_B + 1) * heads * hd]
    else:
        o_b, k_new = _attn_sample_call(z, cache_k, cache_v, bias_tab, q_norm_w, k_norm_w, heads)
        keep = seq
        v_new = z[:, :, UNIT_V_B * heads * hd:(UNIT_V_B + 1) * heads * hd]
    k_new = k_new.reshape(batch, keep, heads, hd)
    v_new = v_new.reshape(batch, keep, heads, hd).astype(F32)

    y = _out_call(o_a, o_b, z, x, gate, w_proj_a, w_proj_b, w_out, heads)
    return y, new_buf, s_new, k_new, v_new


def kernel(x_prompt, x_sample, c_prompt, c_sample, state_conv, state_delta, cache_k, cache_v, norm_w, ada_w, ada_b, w_in, conv_w, gdn_a_log, gdn_dt_bias, gdn_norm_w, q_norm_w, k_norm_w, rel_bias, w_proj_a, w_proj_b, w_out):
    bp, _, d = x_prompt.shape
    depth = w_in.shape[0]
    heads = gdn_a_log.shape[1]
    hd = HEAD_DIM
    small0 = 4 * heads * hd
    assert w_in.shape[2] == 8 * heads * hd + 2 * heads + 2 * d
    assert conv_w.shape[1:] == (CONV_WIDTH, 3 * heads * hd) and rel_bias.shape[1:] == (heads, N_REL)

    mod = _mod_call(jnp.concatenate([c_prompt, c_sample], axis=0), ada_w, ada_b)
    bias_tab = _bias_call(rel_bias)

    yp, ys = x_prompt, x_sample
    outs_p, outs_s = [], []
    zero_buf = jnp.zeros((bp, CONV_WIDTH - 1, 3 * heads * hd), x_prompt.dtype)
    zero_state = jnp.zeros((bp, heads, hd, hd), state_delta.dtype)
    for l in range(depth):
        w = w_in[l]
        u = heads * hd
        after = small0 + 2 * heads
        w_main = jnp.concatenate([w[:, 0:u], w[:, 3 * u:4 * u], w[:, u:2 * u], w[:, after:after + u],
                                  w[:, 2 * u:3 * u], w[:, after + u:after + 2 * u], w[:, after + 2 * u:]],
                                 axis=1).astype(BF16)
        w_small = jnp.pad(w[:, small0:small0 + 2 * heads], ((0, 0), (0, LANES - 2 * heads))).astype(BF16)
        p = (norm_w[l], w_main, w_small, conv_w[l], gdn_a_log[l], gdn_dt_bias[l], gdn_norm_w[l],
             q_norm_w[l], k_norm_w[l], w_proj_a[l].astype(BF16), w_proj_b[l].astype(BF16),
             w_out[l].astype(BF16))
        yp, *rest = _layer(yp, mod[l, :bp], zero_buf, zero_state, None, None, p, bias_tab[l], heads, True)
        outs_p.append(rest)
        ys, *rest = _layer(ys, mod[l, bp:], state_conv[l], state_delta[l], cache_k[l], cache_v[l],
                           p, bias_tab[l], heads, False)
        outs_s.append(rest)

    stack = lambda outs, i: jnp.stack([o[i] for o in outs])
    return (yp, ys,
            stack(outs_p, 0), stack(outs_p, 1), stack(outs_p, 2), stack(outs_p, 3),
            stack(outs_s, 0), stack(outs_s, 1), stack(outs_s, 2), stack(outs_s, 3))
```

```python
import functools
import math

import numpy as np
import jax
import jax.numpy as jnp
from jax import lax
from jax.experimental import pallas as pl
from jax.experimental.pallas import tpu as pltpu

F32 = jnp.float32
BF16 = jnp.bfloat16

EPS = 1e-6
CHUNK = 64
BAND_PREV = 8
REL_CLIP = 128
N_REL = 2 * REL_CLIP + 1
PAST_LEN = 2048
CONV_WIDTH = 4
HEAD_DIM = 128
LANES = 128
SUBLANES = 8
VMEM_LIMIT_BYTES = 56 * 1024 * 1024
GDN_BLOCK = 64
GDN_SUPER = 256
GDN_HEADS_PER_STEP = 8
PREV_ROWS = BAND_PREV * CHUNK
BAND = PREV_ROWS + CHUNK
PAIR_ROWS = 2 * CHUNK
PAIR_COLS = PREV_ROWS + PAIR_ROWS
MASKED = -1e30
ATT_HEADS_PER_STEP = 4


def _dot(a, b):
    return jnp.dot(a, b, preferred_element_type=F32)


def _dot_nt(a, b):
    return lax.dot_general(a, b, (((1,), (1,)), ((), ())), preferred_element_type=F32)


def _dot_tn(a, b):
    return lax.dot_general(a, b, (((0,), (0,)), ((), ())), preferred_element_type=F32)


def _sigmoid(x):
    return 0.5 + 0.5 * jnp.tanh(0.5 * x)


def _silu(x):
    half = 0.5 * x
    return half + half * jnp.tanh(half)


def _rms(x, w):
    return x * lax.rsqrt(jnp.mean(x * x, axis=-1, keepdims=True) + EPS) * w


def _params(*sem):
    return pltpu.CompilerParams(dimension_semantics=sem, vmem_limit_bytes=VMEM_LIMIT_BYTES)


def _largest_divisor(n, candidates):
    for c in candidates:
        if n % c == 0:
            return c
    raise ValueError(f"no tile in {candidates} divides {n}")


def _row_tiling(batch, seq, rows):
    if seq >= rows:
        assert seq % rows == 0
        return 1, rows
    bb = min(batch, rows // seq)
    assert batch % bb == 0
    return bb, seq


def _mod_kernel(c_ref, w_ref, b_ref, o_ref):
    @pl.when(pl.program_id(1) == 0)
    def _():
        o_ref[0] = jnp.broadcast_to(b_ref[0], o_ref.shape[1:])

    o_ref[0] += _dot(_silu(c_ref[...]).astype(BF16), w_ref[0].astype(BF16))


def _mod_call(c, ada_w, ada_b):
    depth, d, n = ada_w.shape
    rows = c.shape[0]
    tk = _largest_divisor(d, (256, 128))
    return pl.pallas_call(
        _mod_kernel,
        grid=(depth, d // tk),
        in_specs=[pl.BlockSpec((rows, tk), lambda l, k: (0, k)),
                  pl.BlockSpec((1, tk, n), lambda l, k: (l, k, 0)),
                  pl.BlockSpec((1, 1, n), lambda l, k: (l, 0, 0))],
        out_specs=pl.BlockSpec((1, rows, n), lambda l, k: (l, 0, 0)),
        out_shape=jax.ShapeDtypeStruct((depth, rows, n), F32),
        compiler_params=_params("parallel", "arbitrary"),
        name="adaln_mod",
    )(c, ada_w, ada_b.reshape(depth, 1, n))


def _bias_kernel(rb_ref, o_ref):
    rb = rb_ref[0]
    heads, nrel = rb.shape
    rb_hi = rb.astype(BF16)
    rest = rb - rb_hi.astype(F32)
    rb_mid = rest.astype(BF16)
    rb_lo = (rest - rb_mid.astype(F32)).astype(BF16)
    r = lax.broadcasted_iota(jnp.int32, (nrel, PAIR_COLS), 0)
    j = lax.broadcasted_iota(jnp.int32, (nrel, PAIR_COLS), 1)
    key_chunk = lax.broadcasted_iota(jnp.int32, (heads, PAIR_COLS), 1) // CHUNK

    def body(i, carry):
        dist = jnp.clip(PREV_ROWS + i - j, -REL_CLIP, REL_CLIP) + REL_CLIP
        onehot = (dist == r).astype(BF16)
        row = (_dot(rb_hi, onehot) + _dot(rb_mid, onehot)) + _dot(rb_lo, onehot)
        q_chunk = i // CHUNK
        in_band = (key_chunk >= q_chunk) & (key_chunk <= q_chunk + BAND_PREV)
        o_ref[0, i] = jnp.where(in_band, row, MASKED)
        return carry

    lax.fori_loop(0, PAIR_ROWS, body, 0)


def _bias_call(rel_bias):
    depth, heads, nrel = rel_bias.shape
    nrel_pad = -(-nrel // SUBLANES) * SUBLANES
    rb = jnp.pad(rel_bias, ((0, 0), (0, 0), (0, nrel_pad - nrel)))
    tab = pl.pallas_call(
        _bias_kernel,
        grid=(depth,),
        in_specs=[pl.BlockSpec((1, heads, nrel_pad), lambda l: (l, 0, 0))],
        out_specs=pl.BlockSpec((1, PAIR_ROWS, heads, PAIR_COLS), lambda l: (l, 0, 0, 0)),
        out_shape=jax.ShapeDtypeStruct((depth, PAIR_ROWS, heads, PAIR_COLS), F32),
        compiler_params=_params("parallel"),
        name="rel_bias_table",
    )(rb)
    return jnp.transpose(tab, (0, 2, 1, 3))


def _inproj_kernel(x_ref, sc_ref, sh_ref, nw_ref, w_ref, ws_ref, z_ref, zs_ref, h_ref, *, bb, tt, rc):
    @pl.when(pl.program_id(2) == 0)
    def _():
        for b in range(bb):
            for r0 in range(0, tt, rc):
                x = x_ref[b, r0:r0 + rc, :]
                h = _rms(x, nw_ref[...]) * (1.0 + sc_ref[b]) + sh_ref[b]
                h_ref[b * tt + r0:b * tt + r0 + rc, :] = h.astype(BF16)
        zs_ref[...] = _dot(h_ref[...], ws_ref[...]).reshape(zs_ref.shape)

    z_ref[...] = _dot(h_ref[...], w_ref[...]).astype(z_ref.dtype).reshape(z_ref.shape)


def _inproj_call(x, scale, shift, norm_w, w_main, w_small, layer):
    batch, seq, d = x.shape
    n = w_main.shape[2]
    ns = w_small.shape[2]
    bb, tt = _row_tiling(batch, seq, 1024)
    tn = _largest_divisor(n, (2048, 1024, 512, 256, 128))
    kern = functools.partial(_inproj_kernel, bb=bb, tt=tt, rc=min(tt, 256))
    return pl.pallas_call(
        kern,
        grid=(batch // bb, seq // tt, n // tn),
        in_specs=[pl.BlockSpec((bb, tt, d), lambda b, i, j: (b, i, 0)),
                  pl.BlockSpec((bb, 1, d), lambda b, i, j: (b, 0, 0)),
                  pl.BlockSpec((bb, 1, d), lambda b, i, j: (b, 0, 0)),
                  pl.BlockSpec((1, d), lambda b, i, j: (0, 0)),
                  pl.BlockSpec((None, d, tn), lambda b, i, j: (layer, 0, j)),
                  pl.BlockSpec((None, d, ns), lambda b, i, j: (layer, 0, 0))],
        out_specs=[pl.BlockSpec((bb, tt, tn), lambda b, i, j: (b, i, j)),
                   pl.BlockSpec((bb, tt, ns), lambda b, i, j: (b, i, 0))],
        out_shape=[jax.ShapeDtypeStruct((batch, seq, n), BF16),
                   jax.ShapeDtypeStruct((batch, seq, ns), F32)],
        scratch_shapes=[pltpu.VMEM((bb * tt, d), BF16)],
        compiler_params=_params("parallel", "parallel", "arbitrary"),
        name="in_proj",
    )(x, scale, shift, norm_w.reshape(1, d), w_main, w_small)


def _gdn_kernel(alog_ref, dt_ref, q_ref, k_ref, v_ref, gate_ref, r_ref,
                cwq_ref, cwk_ref, cwv_ref, csq_ref, csk_ref, csv_ref, s0_ref, nw_ref,
                o_ref, sfin_ref, xs_ref, s_ref, *, tb, blk, hg):
    head0 = pl.program_id(1) * hg
    t = pl.program_id(2)
    pad = SUBLANES
    taps = CONV_WIDTH
    hd = HEAD_DIM
    heads = range(hg)

    @pl.when(t == 0)
    def _():
        xs_ref[0, 0:pad, :] = csq_ref[0]
        xs_ref[1, 0:pad, :] = csk_ref[0]
        xs_ref[2, 0:pad, :] = csv_ref[0]
        s_ref[...] = s0_ref[0]

    xs_ref[0, pad:pad + tb, :] = q_ref[0].astype(F32)
    xs_ref[1, pad:pad + tb, :] = k_ref[0].astype(F32)
    xs_ref[2, pad:pad + tb, :] = v_ref[0].astype(F32)

    def conv_act(i, cw_ref, h):
        cols = slice(h * hd, (h + 1) * hd)
        w = cw_ref[:, cols]
        base = pad - (taps - 1)
        acc = xs_ref[i, base:base + tb, cols] * w[0:1]
        for j in range(1, taps):
            acc = acc + xs_ref[i, base + j:base + j + tb, cols] * w[j:j + 1]
        return _silu(acc)

    lt = min(LANES, tb)
    nblk = tb // blk

    def slab_of(mat, b):
        c0 = (b * blk) // lt * lt
        return mat[b * blk:(b + 1) * blk, c0:c0 + lt]

    def full_of(slabs):
        rows = []
        for b, sl in enumerate(slabs):
            tile = (b * blk) // lt
            parts = [sl if c == tile else jnp.zeros((blk, lt), sl.dtype) for c in range(tb // lt)]
            rows.append(parts[0] if len(parts) == 1 else jnp.concatenate(parts, axis=1))
        return rows[0] if len(rows) == 1 else jnp.concatenate(rows, axis=0)

    incl, strict, eye = [], [], []
    for b in range(nblk):
        ri = lax.broadcasted_iota(jnp.int32, (blk, lt), 0) + b * blk
        ci = lax.broadcasted_iota(jnp.int32, (blk, lt), 1) + (b * blk) // lt * lt
        same = (ci >= b * blk) & (ci < (b + 1) * blk)
        incl.append(same & (ri >= ci))
        strict.append(same & (ri > ci))
        eye.append((ri == ci).astype(F32))

    ri = lax.broadcasted_iota(jnp.int32, (tb, tb), 0)
    ci = lax.broadcasted_iota(jnp.int32, (tb, tb), 1)
    shift = int(math.log2(blk))
    same = (ri >> shift) == (ci >> shift)
    sum_mats = jnp.concatenate([(same & (ri <= ci)).astype(BF16), same.astype(BF16)], axis=1)
    row_id = lax.broadcasted_iota(jnp.int32, (SUBLANES, tb), 0)

    def split16(x):
        hi = x.astype(BF16)
        return hi, (x - hi.astype(F32)).astype(BF16)

    r = r_ref[0].reshape(hg * SUBLANES, tb)
    dt_rows = jnp.concatenate([jnp.full((SUBLANES, tb), dt_ref[head0 + h], F32) for h in heads], axis=0)
    alog_rows = jnp.concatenate([jnp.full((SUBLANES, tb), alog_ref[head0 + h], F32) for h in heads], axis=0)
    beta_rows = _sigmoid(r)
    xx = r + dt_rows
    softplus = jnp.maximum(xx, 0.0) + jnp.log1p(jnp.exp(-jnp.abs(xx)))
    g_rows = -jnp.exp(alog_rows) * softplus
    g_hi = g_rows.astype(BF16)
    g_rest = g_rows - g_hi.astype(F32)
    g_mid = g_rest.astype(BF16)
    g_lo = (g_rest - g_mid.astype(F32)).astype(BF16)
    sums = (_dot(g_hi, sum_mats) + _dot(g_mid, sum_mats)) + _dot(g_lo, sum_mats)

    q, k, v, neg_m, a_intra, beta_c, g_c, g_last_c = [], [], [], [], [], [], [], []
    for h in heads:
        aq = conv_act(0, cwq_ref, h)
        ak = conv_act(1, cwk_ref, h)
        v.append(conv_act(2, cwv_ref, h))
        q.append(aq * lax.rsqrt(jnp.sum(aq * aq, axis=-1, keepdims=True) + EPS) * (hd ** -0.5))
        k.append(ak * lax.rsqrt(jnp.sum(ak * ak, axis=-1, keepdims=True) + EPS))

        hs = slice(h * SUBLANES, (h + 1) * SUBLANES)
        g_cum_rows = sums[hs, :tb]
        cols = jnp.transpose(jnp.where(row_id == 0, beta_rows[hs], g_cum_rows))
        beta_c.append(cols[:, 0:1])
        g_c.append(cols[:, 1:2])
        g_last_c.append(jnp.transpose(sums[hs, tb:])[:, 1:2])
        g_r = g_cum_rows[1:2, :]

        k16 = k[h].astype(BF16)
        kk = _dot_nt(k16, k16)
        qk = _dot_nt(q[h].astype(BF16), k16)
        neg_m_h, a_h = [], []
        for b in range(nblk):
            rs = slice(b * blk, (b + 1) * blk)
            c0 = (b * blk) // lt * lt
            diff = g_c[h][rs] - g_r[:, c0:c0 + lt]
            decay = jnp.where(incl[b], jnp.exp(jnp.where(incl[b], diff, 0.0)), 0.0)
            neg_m_h.append(jnp.where(strict[b], -(beta_c[h][rs] * slab_of(kk, b)) * decay, 0.0))
            a_h.append((slab_of(qk, b) * decay).astype(BF16))
        neg_m.append(neg_m_h)
        a_intra.append(full_of(a_h))

    for i in range(3):
        xs_ref[i, 0:pad, :] = xs_ref[i, tb:tb + pad, :]

    n_hi, n_lo = [], []
    for h in heads:
        pieces = [split16(m) for m in neg_m[h]]
        n_hi.append(full_of([p[0] for p in pieces]))
        n_lo.append(full_of([p[1] for p in pieces]))
    x = [[eye[b] + neg_m[h][b] for b in range(nblk)] for h in heads]
    steps = shift - 1
    for step in range(steps):
        if step < steps - 1:
            x16 = [full_of([xb.astype(BF16) for xb in x[h]]) for h in heads]
            nx = [_dot(n_hi[h], x16[h]) for h in heads]
        else:
            pieces = [[split16(xb) for xb in x[h]] for h in heads]
            x16 = [full_of([p[0] for p in pieces[h]]) for h in heads]
            x_lo = [full_of([p[1] for p in pieces[h]]) for h in heads]
            nx = [_dot(n_hi[h], x16[h]) + (_dot(n_hi[h], x_lo[h]) + _dot(n_lo[h], x16[h])) for h in heads]
        resid = [full_of([((eye[b] - x[h][b]) + slab_of(nx[h], b)).astype(BF16) for b in range(nblk)])
                 for h in heads]
        upd = [_dot(x16[h], resid[h]) for h in heads]
        x = [[x[h][b] + slab_of(upd[h], b) for b in range(nblk)] for h in heads]
    x = [full_of([xb.astype(BF16) for xb in x[h]]) for h in heads]

    u, w, qg, kg, e_last = [], [], [], [], []
    for h in heads:
        e_g = jnp.exp(g_c[h])
        rhs = jnp.concatenate([v[h] * beta_c[h], k[h] * (beta_c[h] * e_g)], axis=1).astype(BF16)
        uw = _dot(x[h], rhs)
        u.append(uw[:, :hd])
        w.append(uw[:, hd:])
        qg.append(q[h] * e_g)
        kg.append((k[h] * jnp.exp(g_last_c[h] - g_c[h])).astype(BF16))
        e_last.append(jnp.exp(g_last_c[h]))

    s = [s_ref[h] for h in heads]
    v_new = [[] for _ in heads]
    o_inter = [[] for _ in heads]
    for b in range(tb // blk):
        rs = slice(b * blk, (b + 1) * blk)
        for h in heads:
            ws_qs = _dot(jnp.concatenate([w[h][rs], qg[h][rs]], axis=0).astype(BF16), s[h].astype(BF16))
            vn = u[h][rs] - ws_qs[:blk]
            v_new[h].append(vn)
            o_inter[h].append(ws_qs[blk:])
            s[h] = s[h] * e_last[h][b * blk:b * blk + 1, :] + _dot_tn(kg[h][rs], vn.astype(BF16))

    for h in heads:
        s_ref[h] = s[h]
        cols = slice(h * hd, (h + 1) * hd)
        vn = jnp.concatenate(v_new[h], axis=0).astype(BF16)
        o = jnp.concatenate(o_inter[h], axis=0) + _dot(a_intra[h], vn)
        o_ref[0, :, cols] = (_rms(o, nw_ref[...]) * _silu(gate_ref[0, :, cols].astype(F32))).astype(BF16)

    @pl.when(t == pl.num_programs(2) - 1)
    def _():
        sfin_ref[0] = s_ref[...]


def _gdn_call(z, rows, conv_w8, conv_buf8, s0, a_log, dt_bias, norm_w, heads):
    batch, seq, _ = z.shape
    tb = min(seq, GDN_SUPER)
    blk = min(tb, GDN_BLOCK)
    hg = min(heads, GDN_HEADS_PER_STEP)
    assert seq % tb == 0 and tb % blk == 0 and heads % hg == 0
    hd = HEAD_DIM
    wd = hg * hd
    ng = heads // hg
    smem = pl.BlockSpec(memory_space=pltpu.SMEM)

    def zcol(seg):
        return pl.BlockSpec((1, tb, wd), lambda b, g, t: (b, t, seg * ng + g))

    def cw(seg):
        return pl.BlockSpec((SUBLANES, wd), lambda b, g, t: (0, seg * ng + g))

    def cs(seg):
        return pl.BlockSpec((1, SUBLANES, wd), lambda b, g, t: (b, 0, seg * ng + g))

    state_spec = pl.BlockSpec((1, hg, hd, hd), lambda b, g, t: (b, g, 0, 0))
    kern = functools.partial(_gdn_kernel, tb=tb, blk=blk, hg=hg)
    return pl.pallas_call(
        kern,
        grid=(batch, ng, seq // tb),
        in_specs=[smem, smem,
                  zcol(0), zcol(1), zcol(2), zcol(3),
                  pl.BlockSpec((1, hg, SUBLANES, tb), lambda b, g, t: (b, g, 0, t)),
                  cw(0), cw(1), cw(2),
                  cs(0), cs(1), cs(2),
                  state_spec,
                  pl.BlockSpec((1, hd), lambda b, g, t: (0, 0))],
        out_specs=[pl.BlockSpec((1, tb, wd), lambda b, g, t: (b, t, g)),
                   state_spec],
        out_shape=[jax.ShapeDtypeStruct((batch, seq, heads * hd), BF16),
                   jax.ShapeDtypeStruct((batch, heads, hd, hd), F32)],
        scratch_shapes=[pltpu.VMEM((3, tb + SUBLANES, wd), F32),
                        pltpu.VMEM((hg, hd, hd), F32)],
        compiler_params=_params("parallel", "parallel", "arbitrary"),
        name="gated_deltanet",
    )(a_log, dt_bias, z, z, z, z, rows, conv_w8, conv_w8, conv_w8,
      conv_buf8, conv_buf8, conv_buf8, s0, norm_w.reshape(1, hd))


def _attn_prompt_kernel(q_ref, k_ref, v_ref, gate_ref, bias_ref, qw_ref, kw_ref,
                        o_ref, kc_ref, kbuf, vbuf, qs, bias_s, *, tq, hg):
    i = pl.program_id(2)
    hd = HEAD_DIM
    npairs = tq // PAIR_ROWS
    problems = [(h, p) for h in range(hg) for p in range(npairs)]

    for h in range(hg):
        cols = slice(h * hd, (h + 1) * hd)

        @pl.when(i == 0)
        def _():
            kbuf[h, 0:PREV_ROWS, :] = jnp.zeros((PREV_ROWS, hd), BF16)
            vbuf[h, 0:PREV_ROWS, :] = jnp.zeros((PREV_ROWS, hd), BF16)

        @pl.when(i > 0)
        def _():
            kbuf[h, 0:PREV_ROWS, :] = kbuf[h, tq:tq + PREV_ROWS, :]
            vbuf[h, 0:PREV_ROWS, :] = vbuf[h, tq:tq + PREV_ROWS, :]

        kn = _rms(k_ref[0, :, cols].astype(F32), kw_ref[...])
        kc_ref[0, :, cols] = kn
        kbuf[h, PREV_ROWS:PREV_ROWS + tq, :] = kn.astype(BF16)
        vbuf[h, PREV_ROWS:PREV_ROWS + tq, :] = v_ref[0, :, cols]
        qs[h] = (_rms(q_ref[0, :, cols].astype(F32), qw_ref[...]) * (hd ** -0.5)).astype(BF16)

    @pl.when(i == 0)
    def _():
        col = lax.broadcasted_iota(jnp.int32, (PAIR_ROWS, PAIR_COLS), 1)
        for h, p in problems:
            bias_s[h, p] = jnp.where(col < PREV_ROWS - p * PAIR_ROWS, MASKED, bias_ref[h])

    @pl.when(i == 1)
    def _():
        for h, p in problems:
            bias_s[h, p] = bias_ref[h]

    def window(p):
        return slice(p * PAIR_ROWS, p * PAIR_ROWS + PAIR_COLS)

    def rows(p):
        return slice(p * PAIR_ROWS, (p + 1) * PAIR_ROWS)

    s = [_dot_nt(qs[h, rows(p), :], kbuf[h, window(p), :]) + bias_s[h, p] for h, p in problems]
    e = [jnp.exp(x - jnp.max(x, axis=-1, keepdims=True)) for x in s]
    denom = [jnp.sum(x, axis=-1, keepdims=True) for x in e]
    for n, (h, p) in enumerate(problems):
        cols = slice(h * hd, (h + 1) * hd)
        o = _dot(e[n].astype(BF16), vbuf[h, window(p), :]) / denom[n]
        o_ref[0, rows(p), cols] = (o * _silu(gate_ref[0, rows(p), cols].astype(F32))).astype(BF16)


def _attn_prompt_call(z, bias_tab, q_norm_w, k_norm_w, heads):
    batch, seq, _ = z.shape
    tq = PREV_ROWS
    hg = min(heads, ATT_HEADS_PER_STEP)
    assert seq % tq == 0 and tq % PAIR_ROWS == 0 and heads % hg == 0
    hd = HEAD_DIM
    wd = hg * hd
    ng = heads // hg

    def zcol(seg):
        return pl.BlockSpec((1, tq, wd), lambda b, g, i: (b, i, seg * ng + g))

    kern = functools.partial(_attn_prompt_kernel, tq=tq, hg=hg)
    return pl.pallas_call(
        kern,
        grid=(batch, ng, seq // tq),
        in_specs=[zcol(4), zcol(5), zcol(6), zcol(7),
                  pl.BlockSpec((hg, PAIR_ROWS, PAIR_COLS), lambda b, g, i: (g, 0, 0)),
                  pl.BlockSpec((1, hd), lambda b, g, i: (0, 0)),
                  pl.BlockSpec((1, hd), lambda b, g, i: (0, 0))],
        out_specs=[pl.BlockSpec((1, tq, wd), lambda b, g, i: (b, i, g)),
                   pl.BlockSpec((1, tq, wd), lambda b, g, i: (b, 0, g))],
        out_shape=[jax.ShapeDtypeStruct((batch, seq, heads * hd), BF16),
                   jax.ShapeDtypeStruct((batch, tq, heads * hd), F32)],
        scratch_shapes=[pltpu.VMEM((hg, PREV_ROWS + tq, hd), BF16),
                        pltpu.VMEM((hg, PREV_ROWS + tq, hd), BF16),
                        pltpu.VMEM((hg, tq, hd), BF16),
                        pltpu.VMEM((hg, tq // PAIR_ROWS, PAIR_ROWS, PAIR_COLS), F32)],
        compiler_params=_params("parallel", "parallel", "arbitrary"),
        name="band_attention_prompt",
    )(z, z, z, z, bias_tab, q_norm_w.reshape(1, hd), k_norm_w.reshape(1, hd))


def _attn_sample_kernel(q_ref, k_ref, v_ref, gate_ref, ck_ref, cv_ref, bias_ref, qw_ref, kw_ref,
                        o_ref, kn_ref, *, seq, n_cached, heads):
    hd = HEAD_DIM
    for h in range(heads):
        cols = slice(h * hd, (h + 1) * hd)
        kn = _rms(k_ref[0, :, cols].astype(F32), kw_ref[...])
        kn_ref[0, :, cols] = kn
        q16 = (_rms(q_ref[0, :, cols].astype(F32), qw_ref[...]) * (hd ** -0.5)).astype(BF16)
        bias = bias_ref[h]
        s_old = _dot_nt(q16, ck_ref[0, :, cols].astype(BF16)) + bias[:seq, :n_cached]
        s_new = _dot_nt(q16, kn.astype(BF16)) + bias[:seq, n_cached:n_cached + seq]
        m = jnp.maximum(jnp.max(s_old, axis=-1, keepdims=True), jnp.max(s_new, axis=-1, keepdims=True))
        p_old = jnp.exp(s_old - m)
        p_new = jnp.exp(s_new - m)
        denom = jnp.sum(p_old, axis=-1, keepdims=True) + jnp.sum(p_new, axis=-1, keepdims=True)
        o = (_dot(p_old.astype(BF16), cv_ref[0, :, cols].astype(BF16))
             + _dot(p_new.astype(BF16), v_ref[0, :, cols])) / denom
        o_ref[0, :, cols] = (o * _silu(gate_ref[0, :, cols].astype(F32))).astype(BF16)


def _attn_sample_call(z, cache_k, cache_v, bias_tab, q_norm_w, k_norm_w, heads):
    batch, seq, _ = z.shape
    n_cached = cache_k.shape[1]
    hd = HEAD_DIM
    wd = heads * hd
    q_pos = PAST_LEN + np.arange(seq)
    k_pos = PAST_LEN - n_cached + np.arange(n_cached + seq)
    qc, kc = q_pos // CHUNK, k_pos // CHUNK
    valid = (k_pos[None] >= 0) & (kc[None] <= qc[:, None]) & (kc[None] >= qc[:, None] - BAND_PREV)
    assert valid.all() and n_cached == PREV_ROWS and seq <= CHUNK and n_cached + seq <= BAND

    def zcol(seg):
        return pl.BlockSpec((1, seq, wd), lambda b: (b, 0, seg))

    cache_spec = pl.BlockSpec((1, n_cached, wd), lambda b: (b, 0, 0))
    kern = functools.partial(_attn_sample_kernel, seq=seq, n_cached=n_cached, heads=heads)
    return pl.pallas_call(
        kern,
        grid=(batch,),
        in_specs=[zcol(4), zcol(5), zcol(6), zcol(7),
                  cache_spec, cache_spec,
                  pl.BlockSpec((heads, PAIR_ROWS, PAIR_COLS), lambda b: (0, 0, 0)),
                  pl.BlockSpec((1, hd), lambda b: (0, 0)),
                  pl.BlockSpec((1, hd), lambda b: (0, 0))],
        out_specs=[pl.BlockSpec((1, seq, wd), lambda b: (b, 0, 0)),
                   pl.BlockSpec((1, seq, wd), lambda b: (b, 0, 0))],
        out_shape=[jax.ShapeDtypeStruct((batch, seq, wd), BF16),
                   jax.ShapeDtypeStruct((batch, seq, wd), F32)],
        compiler_params=_params("parallel"),
        name="band_attention_sample",
    )(z, z, z, z, cache_k.reshape(batch, n_cached, wd),
      cache_v.reshape(batch, n_cached, wd), bias_tab,
      q_norm_w.reshape(1, hd), k_norm_w.reshape(1, hd))


def _out_kernel(oa_ref, ob_ref, ma_ref, mb_ref, x_ref, g_ref, wa_ref, wb_ref, wo_ref, y_ref):
    bb, tt, d = x_ref.shape
    ya = _dot(oa_ref[...].reshape(bb * tt, -1), wa_ref[...])
    yb = _dot(ob_ref[...].reshape(bb * tt, -1), wb_ref[...])
    merged = (_sigmoid(ma_ref[...].reshape(bb * tt, d).astype(F32)) * ya
              + _sigmoid(mb_ref[...].reshape(bb * tt, d).astype(F32)) * yb)
    delta = _dot(merged.astype(BF16), wo_ref[...]).reshape(bb, tt, d)
    y_ref[...] = x_ref[...] + g_ref[...] * delta


def _out_call(o_a, o_b, z, x, gate, w_proj_a, w_proj_b, w_out, heads, layer):
    batch, seq, d = x.shape
    bb, tt = _row_tiling(batch, seq, 512)
    wa = o_a.shape[-1]
    wb = o_b.shape[-1]
    m_off = (8 * heads * HEAD_DIM) // d
    assert m_off * d == 8 * heads * HEAD_DIM

    def rows(width, off=0):
        return pl.BlockSpec((bb, tt, width), lambda b, i: (b, i, off))

    def whole(shape):
        return pl.BlockSpec((None,) + shape, lambda b, i: (layer, 0, 0), pipeline_mode=pl.Buffered(1))

    return pl.pallas_call(
        _out_kernel,
        grid=(batch // bb, seq // tt),
        in_specs=[rows(wa), rows(wb), rows(d, m_off), rows(d, m_off + 1), rows(d),
                  pl.BlockSpec((bb, 1, d), lambda b, i: (b, 0, 0)),
                  whole((wa, d)), whole((wb, d)), whole((d, d))],
        out_specs=rows(d),
        out_shape=jax.ShapeDtypeStruct((batch, seq, d), F32),
        compiler_params=_params("parallel", "parallel"),
        name="out_proj",
    )(o_a, o_b, z, z, x, gate, w_proj_a, w_proj_b, w_out)


def _layer(x, mod, conv_buf, s0, cache_k, cache_v, p, bias_tab, heads, prompt, layer):
    (norm_w, w_main, w_small, conv_w8, a_log, dt_bias, gdn_norm_w,
     q_norm_w, k_norm_w, w_proj_a, w_proj_b, w_out) = p
    batch, seq, d = x.shape
    hd = HEAD_DIM
    shift, scale, gate = [m.reshape(batch, 1, d) for m in jnp.split(mod, 3, axis=-1)]

    z, z_small = _inproj_call(x, scale, shift, norm_w, w_main, w_small, layer)

    logits = jnp.transpose(z_small[..., :2 * heads].reshape(batch, seq, 2, heads), (0, 3, 2, 1))
    rows = jnp.pad(logits, ((0, 0), (0, 0), (0, SUBLANES - 2), (0, 0)))
    conv_buf8 = jnp.pad(conv_buf, ((0, 0), (SUBLANES - (CONV_WIDTH - 1), 0), (0, 0)))
    o_a, s_new = _gdn_call(z, rows, conv_w8, conv_buf8, s0, a_log, dt_bias, gdn_norm_w, heads)

    conv_dim = 3 * heads * hd
    new_buf = z[:, seq - (CONV_WIDTH - 1):, :conv_dim].astype(F32)
    if prompt:
        o_b, k_new = _attn_prompt_call(z, bias_tab, q_norm_w, k_norm_w, heads)
        keep = k_new.shape[1]
        v_new = z[:, seq - keep:, 6 * heads * hd:7 * heads * hd]
    else:
        o_b, k_new = _attn_sample_call(z, cache_k, cache_v, bias_tab, q_norm_w, k_norm_w, heads)
        keep = seq
        v_new = z[:, :, 6 * heads * hd:7 * heads * hd]
    k_new = k_new.reshape(batch, keep, heads, hd)
    v_new = v_new.reshape(batch, keep, heads, hd).astype(F32)

    y = _out_call(o_a, o_b, z, x, gate, w_proj_a, w_proj_b, w_out, heads, layer)
    return y, new_buf, s_new, k_new, v_new


def kernel(x_prompt, x_sample, c_prompt, c_sample, state_conv, state_delta, cache_k, cache_v, norm_w, ada_w, ada_b, w_in, conv_w, gdn_a_log, gdn_dt_bias, gdn_norm_w, q_norm_w, k_norm_w, rel_bias, w_proj_a, w_proj_b, w_out):
    bp, _, d = x_prompt.shape
    depth = w_in.shape[0]
    heads = gdn_a_log.shape[1]
    hd = HEAD_DIM
    small0 = 4 * heads * hd
    assert w_in.shape[2] == 8 * heads * hd + 2 * heads + 2 * d
    assert conv_w.shape[1:] == (CONV_WIDTH, 3 * heads * hd) and rel_bias.shape[1:] == (heads, N_REL)

    mod = _mod_call(jnp.concatenate([c_prompt, c_sample], axis=0), ada_w, ada_b)
    bias_tab = _bias_call(rel_bias)

    w_main = jnp.concatenate([w_in[:, :, :small0], w_in[:, :, small0 + 2 * heads:]], axis=2).astype(BF16)
    w_small = jnp.pad(w_in[:, :, small0:small0 + 2 * heads],
                      ((0, 0), (0, 0), (0, LANES - 2 * heads))).astype(BF16)
    w_proj_a, w_proj_b, w_out = (w.astype(BF16) for w in (w_proj_a, w_proj_b, w_out))

    yp, ys = x_prompt, x_sample
    outs_p, outs_s = [], []
    zero_buf = jnp.zeros((bp, CONV_WIDTH - 1, 3 * heads * hd), x_prompt.dtype)
    zero_state = jnp.zeros((bp, heads, hd, hd), state_delta.dtype)
    for l in range(depth):
        conv_w8 = jnp.pad(conv_w[l], ((0, SUBLANES - CONV_WIDTH), (0, 0)))
        p = (norm_w[l], w_main, w_small, conv_w8, gdn_a_log[l], gdn_dt_bias[l], gdn_norm_w[l],
             q_norm_w[l], k_norm_w[l], w_proj_a, w_proj_b, w_out)
        yp, *rest = _layer(yp, mod[l, :bp], zero_buf, zero_state, None, None, p, bias_tab[l], heads, True, l)
        outs_p.append(rest)
        ys, *rest = _layer(ys, mod[l, bp:], state_conv[l], state_delta[l], cache_k[l], cache_v[l],
                           p, bias_tab[l], heads, False, l)
        outs_s.append(rest)

    stack = lambda outs, i: jnp.stack([o[i] for o in outs])
    return (yp, ys,
            stack(outs_p, 0), stack(outs_p, 1), stack(outs_p, 2), stack(outs_p, 3),
            stack(outs_s, 0), stack(outs_s, 1), stack(outs_s, 2), stack(outs_s, 3))
```

```python
import functools
import math

import numpy as np
import jax
import jax.numpy as jnp
from jax import lax
from jax.experimental import pallas as pl
from jax.experimental.pallas import tpu as pltpu

F32 = jnp.float32
BF16 = jnp.bfloat16

EPS = 1e-6
CHUNK = 64
BAND_PREV = 8
REL_CLIP = 128
N_REL = 2 * REL_CLIP + 1
PAST_LEN = 2048
CONV_WIDTH = 4
HEAD_DIM = 128
LANES = 128
SUBLANES = 8
VMEM_LIMIT_BYTES = 56 * 1024 * 1024
GDN_BLOCK = 64
GDN_SUPER = 256
GDN_HEADS_PER_STEP = 8
PREV_ROWS = BAND_PREV * CHUNK
BAND = PREV_ROWS + CHUNK
PAIR_ROWS = 2 * CHUNK
PAIR_COLS = PREV_ROWS + PAIR_ROWS
MASKED = -1e30
ATT_HEADS_PER_STEP = 4


def _dot(a, b):
    return jnp.dot(a, b, preferred_element_type=F32)


def _dot_nt(a, b):
    return lax.dot_general(a, b, (((1,), (1,)), ((), ())), preferred_element_type=F32)


def _dot_tn(a, b):
    return lax.dot_general(a, b, (((0,), (0,)), ((), ())), preferred_element_type=F32)


def _sigmoid(x):
    return 0.5 + 0.5 * jnp.tanh(0.5 * x)


def _silu(x):
    half = 0.5 * x
    return half + half * jnp.tanh(half)


def _rms(x, w):
    return x * lax.rsqrt(jnp.mean(x * x, axis=-1, keepdims=True) + EPS) * w


def _params(*sem):
    return pltpu.CompilerParams(dimension_semantics=sem, vmem_limit_bytes=VMEM_LIMIT_BYTES)


def _largest_divisor(n, candidates):
    for c in candidates:
        if n % c == 0:
            return c
    raise ValueError(f"no tile in {candidates} divides {n}")


def _row_tiling(batch, seq, rows):
    if seq >= rows:
        assert seq % rows == 0
        return 1, rows
    bb = min(batch, rows // seq)
    assert batch % bb == 0
    return bb, seq


def _mod_kernel(c_ref, w0_ref, w1_ref, w2_ref, b_ref, o_ref):
    @pl.when(pl.program_id(1) == 0)
    def _():
        o_ref[0] = jnp.broadcast_to(b_ref[0], o_ref.shape[1:])

    act = _silu(c_ref[...]).astype(BF16)
    n3 = w0_ref.shape[-1]
    for t, w_ref in enumerate((w0_ref, w1_ref, w2_ref)):
        o_ref[0, :, t * n3:(t + 1) * n3] += _dot(act, w_ref[0].astype(BF16))


def _mod_call(c, ada_w, ada_b):
    depth, d, n = ada_w.shape
    rows = c.shape[0]
    tk = _largest_divisor(d, (256, 128))
    n3 = n // 3
    assert n3 * 3 == n and n3 % LANES == 0

    def third(t):
        return pl.BlockSpec((1, tk, n3), lambda l, k: (l, k, t))

    return pl.pallas_call(
        _mod_kernel,
        grid=(depth, d // tk),
        in_specs=[pl.BlockSpec((rows, tk), lambda l, k: (0, k)),
                  third(0), third(1), third(2),
                  pl.BlockSpec((1, 1, n), lambda l, k: (l, 0, 0))],
        out_specs=pl.BlockSpec((1, rows, n), lambda l, k: (l, 0, 0)),
        out_shape=jax.ShapeDtypeStruct((depth, rows, n), F32),
        compiler_params=_params("parallel", "arbitrary"),
        name="adaln_mod",
    )(c, ada_w, ada_w, ada_w, ada_b.reshape(depth, 1, n))


def _bias_kernel(rb_ref, o_ref):
    rb = rb_ref[0]
    heads, nrel = rb.shape
    rb_hi = rb.astype(BF16)
    rest = rb - rb_hi.astype(F32)
    rb_mid = rest.astype(BF16)
    rb_lo = (rest - rb_mid.astype(F32)).astype(BF16)
    r = lax.broadcasted_iota(jnp.int32, (nrel, PAIR_COLS), 0)
    j = lax.broadcasted_iota(jnp.int32, (nrel, PAIR_COLS), 1)
    onehot = (jnp.clip(PREV_ROWS - j, -REL_CLIP, REL_CLIP) + REL_CLIP == r).astype(BF16)
    row0 = (_dot(rb_hi, onehot) + _dot(rb_mid, onehot)) + _dot(rb_lo, onehot)
    lane = lax.broadcasted_iota(jnp.int32, (heads, PAIR_COLS), 1)
    key_chunk = lane // CHUNK
    assert PREV_ROWS >= REL_CLIP
    far = rb[:, N_REL - 1:N_REL]

    def body(i, row):
        q_chunk = i // CHUNK
        in_band = (key_chunk >= q_chunk) & (key_chunk <= q_chunk + BAND_PREV)
        o_ref[0, i] = jnp.where(in_band, row, MASKED)
        return jnp.where(lane == 0, far, pltpu.roll(row, 1, axis=1))

    lax.fori_loop(0, PAIR_ROWS, body, row0)


def _bias_call(rel_bias):
    depth, heads, nrel = rel_bias.shape
    nrel_pad = -(-nrel // SUBLANES) * SUBLANES
    rb = jnp.pad(rel_bias, ((0, 0), (0, 0), (0, nrel_pad - nrel)))
    tab = pl.pallas_call(
        _bias_kernel,
        grid=(depth,),
        in_specs=[pl.BlockSpec((1, heads, nrel_pad), lambda l: (l, 0, 0))],
        out_specs=pl.BlockSpec((1, PAIR_ROWS, heads, PAIR_COLS), lambda l: (l, 0, 0, 0)),
        out_shape=jax.ShapeDtypeStruct((depth, PAIR_ROWS, heads, PAIR_COLS), F32),
        compiler_params=_params("parallel"),
        name="rel_bias_table",
    )(rb)
    return jnp.transpose(tab, (0, 2, 1, 3))


def _inproj_kernel(x_ref, sc_ref, sh_ref, nw_ref, w_ref, ws_ref, z_ref, zs_ref, h_ref, *, bb, tt, rc):
    @pl.when(pl.program_id(2) == 0)
    def _():
        for b in range(bb):
            for r0 in range(0, tt, rc):
                x = x_ref[b, r0:r0 + rc, :]
                h = _rms(x, nw_ref[...]) * (1.0 + sc_ref[b]) + sh_ref[b]
                h_ref[b * tt + r0:b * tt + r0 + rc, :] = h.astype(BF16)
        zs_ref[...] = _dot(h_ref[...], ws_ref[...]).reshape(zs_ref.shape)

    z_ref[...] = _dot(h_ref[...], w_ref[...]).astype(z_ref.dtype).reshape(z_ref.shape)


def _inproj_call(x, scale, shift, norm_w, w_main, w_small, layer):
    batch, seq, d = x.shape
    n = w_main.shape[2]
    ns = w_small.shape[2]
    bb, tt = _row_tiling(batch, seq, 1024)
    tn = _largest_divisor(n, (2048, 1024, 512, 256, 128))
    kern = functools.partial(_inproj_kernel, bb=bb, tt=tt, rc=min(tt, 256))
    return pl.pallas_call(
        kern,
        grid=(batch // bb, seq // tt, n // tn),
        in_specs=[pl.BlockSpec((bb, tt, d), lambda b, i, j: (b, i, 0)),
                  pl.BlockSpec((bb, 1, d), lambda b, i, j: (b, 0, 0)),
                  pl.BlockSpec((bb, 1, d), lambda b, i, j: (b, 0, 0)),
                  pl.BlockSpec((1, d), lambda b, i, j: (0, 0)),
                  pl.BlockSpec((None, d, tn), lambda b, i, j: (layer, 0, j)),
                  pl.BlockSpec((None, d, ns), lambda b, i, j: (layer, 0, 0))],
        out_specs=[pl.BlockSpec((bb, tt, tn), lambda b, i, j: (b, i, j)),
                   pl.BlockSpec((bb, tt, ns), lambda b, i, j: (b, i, 0))],
        out_shape=[jax.ShapeDtypeStruct((batch, seq, n), BF16),
                   jax.ShapeDtypeStruct((batch, seq, ns), F32)],
        scratch_shapes=[pltpu.VMEM((bb * tt, d), BF16)],
        compiler_params=_params("parallel", "parallel", "arbitrary"),
        name="in_proj",
    )(x, scale, shift, norm_w.reshape(1, d), w_main, w_small)


def _gdn_kernel(alog_ref, dt_ref, q_ref, k_ref, v_ref, gate_ref, r_ref,
                cwq_ref, cwk_ref, cwv_ref, csq_ref, csk_ref, csv_ref, s0_ref, nw_ref,
                o_ref, sfin_ref, xs_ref, s_ref, *, tb, blk, hg):
    head0 = pl.program_id(1) * hg
    t = pl.program_id(2)
    pad = SUBLANES
    taps = CONV_WIDTH
    hd = HEAD_DIM
    heads = range(hg)

    @pl.when(t == 0)
    def _():
        xs_ref[0, 0:pad, :] = csq_ref[0]
        xs_ref[1, 0:pad, :] = csk_ref[0]
        xs_ref[2, 0:pad, :] = csv_ref[0]
        s_ref[...] = s0_ref[0]

    xs_ref[0, pad:pad + tb, :] = q_ref[0].astype(F32)
    xs_ref[1, pad:pad + tb, :] = k_ref[0].astype(F32)
    xs_ref[2, pad:pad + tb, :] = v_ref[0].astype(F32)

    def conv_act(i, cw_ref, h):
        cols = slice(h * hd, (h + 1) * hd)
        w = cw_ref[:, cols]
        base = pad - (taps - 1)
        acc = xs_ref[i, base:base + tb, cols] * w[0:1]
        for j in range(1, taps):
            acc = acc + xs_ref[i, base + j:base + j + tb, cols] * w[j:j + 1]
        return _silu(acc)

    lt = min(LANES, tb)
    nblk = tb // blk

    def slab_of(mat, b):
        c0 = (b * blk) // lt * lt
        return mat[b * blk:(b + 1) * blk, c0:c0 + lt]

    def full_of(slabs):
        rows = []
        for b, sl in enumerate(slabs):
            tile = (b * blk) // lt
            parts = [sl if c == tile else jnp.zeros((blk, lt), sl.dtype) for c in range(tb // lt)]
            rows.append(parts[0] if len(parts) == 1 else jnp.concatenate(parts, axis=1))
        return rows[0] if len(rows) == 1 else jnp.concatenate(rows, axis=0)

    incl, strict, eye = [], [], []
    for b in range(nblk):
        ri = lax.broadcasted_iota(jnp.int32, (blk, lt), 0) + b * blk
        ci = lax.broadcasted_iota(jnp.int32, (blk, lt), 1) + (b * blk) // lt * lt
        same = (ci >= b * blk) & (ci < (b + 1) * blk)
        incl.append(same & (ri >= ci))
        strict.append(same & (ri > ci))
        eye.append((ri == ci).astype(F32))

    ri = lax.broadcasted_iota(jnp.int32, (tb, tb), 0)
    ci = lax.broadcasted_iota(jnp.int32, (tb, tb), 1)
    shift = int(math.log2(blk))
    same = (ri >> shift) == (ci >> shift)
    sum_mats = jnp.concatenate([(same & (ri <= ci)).astype(BF16), same.astype(BF16)], axis=1)
    row_id = lax.broadcasted_iota(jnp.int32, (SUBLANES, tb), 0)

    def split16(x):
        hi = x.astype(BF16)
        return hi, (x - hi.astype(F32)).astype(BF16)

    r = r_ref[0].reshape(hg * SUBLANES, tb)
    dt_rows = jnp.concatenate([jnp.full((SUBLANES, tb), dt_ref[head0 + h], F32) for h in heads], axis=0)
    alog_rows = jnp.concatenate([jnp.full((SUBLANES, tb), alog_ref[head0 + h], F32) for h in heads], axis=0)
    beta_rows = _sigmoid(r)
    xx = r + dt_rows
    softplus = jnp.maximum(xx, 0.0) + jnp.log1p(jnp.exp(-jnp.abs(xx)))
    g_rows = -jnp.exp(alog_rows) * softplus
    g_hi = g_rows.astype(BF16)
    g_rest = g_rows - g_hi.astype(F32)
    g_mid = g_rest.astype(BF16)
    g_lo = (g_rest - g_mid.astype(F32)).astype(BF16)
    sums = (_dot(g_hi, sum_mats) + _dot(g_mid, sum_mats)) + _dot(g_lo, sum_mats)

    q, k, v, neg_m, a_intra, beta_c, g_c, g_last_c = [], [], [], [], [], [], [], []
    for h in heads:
        aq = conv_act(0, cwq_ref, h)
        ak = conv_act(1, cwk_ref, h)
        v.append(conv_act(2, cwv_ref, h))
        q.append(aq * lax.rsqrt(jnp.sum(aq * aq, axis=-1, keepdims=True) + EPS) * (hd ** -0.5))
        k.append(ak * lax.rsqrt(jnp.sum(ak * ak, axis=-1, keepdims=True) + EPS))

        hs = slice(h * SUBLANES, (h + 1) * SUBLANES)
        g_cum_rows = sums[hs, :tb]
        cols = jnp.transpose(jnp.where(row_id == 0, beta_rows[hs], g_cum_rows))
        beta_c.append(cols[:, 0:1])
        g_c.append(cols[:, 1:2])
        g_last_c.append(jnp.transpose(sums[hs, tb:])[:, 1:2])
        g_r = g_cum_rows[1:2, :]

        k16 = k[h].astype(BF16)
        kk = _dot_nt(k16, k16)
        qk = _dot_nt(q[h].astype(BF16), k16)
        neg_m_h, a_h = [], []
        for b in range(nblk):
            rs = slice(b * blk, (b + 1) * blk)
            c0 = (b * blk) // lt * lt
            diff = g_c[h][rs] - g_r[:, c0:c0 + lt]
            decay = jnp.where(incl[b], jnp.exp(jnp.where(incl[b], diff, 0.0)), 0.0)
            neg_m_h.append(jnp.where(strict[b], -(beta_c[h][rs] * slab_of(kk, b)) * decay, 0.0))
            a_h.append((slab_of(qk, b) * decay).astype(BF16))
        neg_m.append(neg_m_h)
        a_intra.append(full_of(a_h))

    for i in range(3):
        xs_ref[i, 0:pad, :] = xs_ref[i, tb:tb + pad, :]

    n_hi, n_lo = [], []
    for h in heads:
        pieces = [split16(m) for m in neg_m[h]]
        n_hi.append(full_of([p[0] for p in pieces]))
        n_lo.append(full_of([p[1] for p in pieces]))
    x = [[eye[b] + neg_m[h][b] for b in range(nblk)] for h in heads]
    steps = shift - 1
    for step in range(steps):
        if step < steps - 1:
            x16 = [full_of([xb.astype(BF16) for xb in x[h]]) for h in heads]
            nx = [_dot(n_hi[h], x16[h]) for h in heads]
        else:
            pieces = [[split16(xb) for xb in x[h]] for h in heads]
            x16 = [full_of([p[0] for p in pieces[h]]) for h in heads]
            x_lo = [full_of([p[1] for p in pieces[h]]) for h in heads]
            nx = [_dot(n_hi[h], x16[h]) + (_dot(n_hi[h], x_lo[h]) + _dot(n_lo[h], x16[h])) for h in heads]
        resid = [full_of([((eye[b] - x[h][b]) + slab_of(nx[h], b)).astype(BF16) for b in range(nblk)])
                 for h in heads]
        upd = [_dot(x16[h], resid[h]) for h in heads]
        x = [[x[h][b] + slab_of(upd[h], b) for b in range(nblk)] for h in heads]
    x = [full_of([xb.astype(BF16) for xb in x[h]]) for h in heads]

    u, w, qg, kg, e_last = [], [], [], [], []
    for h in heads:
        e_g = jnp.exp(g_c[h])
        rhs = jnp.concatenate([v[h] * beta_c[h], k[h] * (beta_c[h] * e_g)], axis=1).astype(BF16)
        uw = _dot(x[h], rhs)
        u.append(uw[:, :hd])
        w.append(uw[:, hd:])
        qg.append(q[h] * e_g)
        kg.append((k[h] * jnp.exp(g_last_c[h] - g_c[h])).astype(BF16))
        e_last.append(jnp.exp(g_last_c[h]))

    s = [s_ref[h] for h in heads]
    v_new = [[] for _ in heads]
    o_inter = [[] for _ in heads]
    for b in range(tb // blk):
        rs = slice(b * blk, (b + 1) * blk)
        for h in heads:
            ws_qs = _dot(jnp.concatenate([w[h][rs], qg[h][rs]], axis=0).astype(BF16), s[h].astype(BF16))
            vn = u[h][rs] - ws_qs[:blk]
            v_new[h].append(vn)
            o_inter[h].append(ws_qs[blk:])
            s[h] = s[h] * e_last[h][b * blk:b * blk + 1, :] + _dot_tn(kg[h][rs], vn.astype(BF16))

    for h in heads:
        s_ref[h] = s[h]
        cols = slice(h * hd, (h + 1) * hd)
        vn = jnp.concatenate(v_new[h], axis=0).astype(BF16)
        o = jnp.concatenate(o_inter[h], axis=0) + _dot(a_intra[h], vn)
        o_ref[0, :, cols] = (_rms(o, nw_ref[...]) * _silu(gate_ref[0, :, cols].astype(F32))).astype(BF16)

    @pl.when(t == pl.num_programs(2) - 1)
    def _():
        sfin_ref[0] = s_ref[...]


def _gdn_call(z, rows, conv_w8, conv_buf8, s0, a_log, dt_bias, norm_w, heads):
    batch, seq, _ = z.shape
    tb = min(seq, GDN_SUPER)
    blk = min(tb, GDN_BLOCK)
    hg = min(heads, GDN_HEADS_PER_STEP)
    assert seq % tb == 0 and tb % blk == 0 and heads % hg == 0
    hd = HEAD_DIM
    wd = hg * hd
    ng = heads // hg
    smem = pl.BlockSpec(memory_space=pltpu.SMEM)

    def zcol(seg):
        return pl.BlockSpec((1, tb, wd), lambda b, g, t: (b, t, seg * ng + g))

    def cw(seg):
        return pl.BlockSpec((SUBLANES, wd), lambda b, g, t: (0, seg * ng + g))

    def cs(seg):
        return pl.BlockSpec((1, SUBLANES, wd), lambda b, g, t: (b, 0, seg * ng + g))

    state_spec = pl.BlockSpec((1, hg, hd, hd), lambda b, g, t: (b, g, 0, 0))
    kern = functools.partial(_gdn_kernel, tb=tb, blk=blk, hg=hg)
    return pl.pallas_call(
        kern,
        grid=(batch, ng, seq // tb),
        in_specs=[smem, smem,
                  zcol(0), zcol(1), zcol(2), zcol(3),
                  pl.BlockSpec((1, hg, SUBLANES, tb), lambda b, g, t: (b, g, 0, t)),
                  cw(0), cw(1), cw(2),
                  cs(0), cs(1), cs(2),
                  state_spec,
                  pl.BlockSpec((1, hd), lambda b, g, t: (0, 0))],
        out_specs=[pl.BlockSpec((1, tb, wd), lambda b, g, t: (b, t, g)),
                   state_spec],
        out_shape=[jax.ShapeDtypeStruct((batch, seq, heads * hd), BF16),
                   jax.ShapeDtypeStruct((batch, heads, hd, hd), F32)],
        scratch_shapes=[pltpu.VMEM((3, tb + SUBLANES, wd), F32),
                        pltpu.VMEM((hg, hd, hd), F32)],
        compiler_params=_params("parallel", "parallel", "arbitrary"),
        name="gated_deltanet",
    )(a_log, dt_bias, z, z, z, z, rows, conv_w8, conv_w8, conv_w8,
      conv_buf8, conv_buf8, conv_buf8, s0, norm_w.reshape(1, hd))


def _attn_prompt_kernel(q_ref, k_ref, v_ref, gate_ref, bias_ref, qw_ref, kw_ref,
                        o_ref, kc_ref, kbuf, vbuf, qs, bias_s, *, tq, hg):
    i = pl.program_id(2)
    hd = HEAD_DIM
    npairs = tq // PAIR_ROWS
    problems = [(h, p) for h in range(hg) for p in range(npairs)]

    for h in range(hg):
        cols = slice(h * hd, (h + 1) * hd)

        @pl.when(i == 0)
        def _():
            kbuf[h, 0:PREV_ROWS, :] = jnp.zeros((PREV_ROWS, hd), BF16)
            vbuf[h, 0:PREV_ROWS, :] = jnp.zeros((PREV_ROWS, hd), BF16)

        @pl.when(i > 0)
        def _():
            kbuf[h, 0:PREV_ROWS, :] = kbuf[h, tq:tq + PREV_ROWS, :]
            vbuf[h, 0:PREV_ROWS, :] = vbuf[h, tq:tq + PREV_ROWS, :]

        kn = _rms(k_ref[0, :, cols].astype(F32), kw_ref[...])
        kc_ref[0, :, cols] = kn
        kbuf[h, PREV_ROWS:PREV_ROWS + tq, :] = kn.astype(BF16)
        vbuf[h, PREV_ROWS:PREV_ROWS + tq, :] = v_ref[0, :, cols]
        qs[h] = (_rms(q_ref[0, :, cols].astype(F32), qw_ref[...]) * (hd ** -0.5)).astype(BF16)

    @pl.when(i == 0)
    def _():
        col = lax.broadcasted_iota(jnp.int32, (PAIR_ROWS, PAIR_COLS), 1)
        for h, p in problems:
            bias_s[h, p] = jnp.where(col < PREV_ROWS - p * PAIR_ROWS, MASKED, bias_ref[h])

    @pl.when(i == 1)
    def _():
        for h, p in problems:
            bias_s[h, p] = bias_ref[h]

    def window(p):
        return slice(p * PAIR_ROWS, p * PAIR_ROWS + PAIR_COLS)

    def rows(p):
        return slice(p * PAIR_ROWS, (p + 1) * PAIR_ROWS)

    s = [_dot_nt(qs[h, rows(p), :], kbuf[h, window(p), :]) + bias_s[h, p] for h, p in problems]
    e = [jnp.exp(x - jnp.max(x, axis=-1, keepdims=True)) for x in s]
    denom = [jnp.sum(x, axis=-1, keepdims=True) for x in e]
    for n, (h, p) in enumerate(problems):
        cols = slice(h * hd, (h + 1) * hd)
        o = _dot(e[n].astype(BF16), vbuf[h, window(p), :]) / denom[n]
        o_ref[0, rows(p), cols] = (o * _silu(gate_ref[0, rows(p), cols].astype(F32))).astype(BF16)


def _attn_prompt_call(z, bias_tab, q_norm_w, k_norm_w, heads):
    batch, seq, _ = z.shape
    tq = PREV_ROWS
    hg = min(heads, ATT_HEADS_PER_STEP)
    assert seq % tq == 0 and tq % PAIR_ROWS == 0 and heads % hg == 0
    hd = HEAD_DIM
    wd = hg * hd
    ng = heads // hg

    def zcol(seg):
        return pl.BlockSpec((1, tq, wd), lambda b, g, i: (b, i, seg * ng + g))

    kern = functools.partial(_attn_prompt_kernel, tq=tq, hg=hg)
    return pl.pallas_call(
        kern,
        grid=(batch, ng, seq // tq),
        in_specs=[zcol(4), zcol(5), zcol(6), zcol(7),
                  pl.BlockSpec((hg, PAIR_ROWS, PAIR_COLS), lambda b, g, i: (g, 0, 0)),
                  pl.BlockSpec((1, hd), lambda b, g, i: (0, 0)),
                  pl.BlockSpec((1, hd), lambda b, g, i: (0, 0))],
        out_specs=[pl.BlockSpec((1, tq, wd), lambda b, g, i: (b, i, g)),
                   pl.BlockSpec((1, tq, wd), lambda b, g, i: (b, 0, g))],
        out_shape=[jax.ShapeDtypeStruct((batch, seq, heads * hd), BF16),
                   jax.ShapeDtypeStruct((batch, tq, heads * hd), F32)],
        scratch_shapes=[pltpu.VMEM((hg, PREV_ROWS + tq, hd), BF16),
                        pltpu.VMEM((hg, PREV_ROWS + tq, hd), BF16),
                        pltpu.VMEM((hg, tq, hd), BF16),
                        pltpu.VMEM((hg, tq // PAIR_ROWS, PAIR_ROWS, PAIR_COLS), F32)],
        compiler_params=_params("parallel", "parallel", "arbitrary"),
        name="band_attention_prompt",
    )(z, z, z, z, bias_tab, q_norm_w.reshape(1, hd), k_norm_w.reshape(1, hd))


def _attn_sample_kernel(q_ref, k_ref, v_ref, gate_ref, ck_ref, cv_ref, bias_ref, qw_ref, kw_ref,
                        o_ref, kn_ref, *, seq, n_cached, heads):
    hd = HEAD_DIM
    for h in range(heads):
        cols = slice(h * hd, (h + 1) * hd)
        kn = _rms(k_ref[0, :, cols].astype(F32), kw_ref[...])
        kn_ref[0, :, cols] = kn
        q16 = (_rms(q_ref[0, :, cols].astype(F32), qw_ref[...]) * (hd ** -0.5)).astype(BF16)
        bias = bias_ref[h]
        s_old = _dot_nt(q16, ck_ref[0, :, cols].astype(BF16)) + bias[:seq, :n_cached]
        s_new = _dot_nt(q16, kn.astype(BF16)) + bias[:seq, n_cached:n_cached + seq]
        m = jnp.maximum(jnp.max(s_old, axis=-1, keepdims=True), jnp.max(s_new, axis=-1, keepdims=True))
        p_old = jnp.exp(s_old - m)
        p_new = jnp.exp(s_new - m)
        denom = jnp.sum(p_old, axis=-1, keepdims=True) + jnp.sum(p_new, axis=-1, keepdims=True)
        o = (_dot(p_old.astype(BF16), cv_ref[0, :, cols].astype(BF16))
             + _dot(p_new.astype(BF16), v_ref[0, :, cols])) / denom
        o_ref[0, :, cols] = (o * _silu(gate_ref[0, :, cols].astype(F32))).astype(BF16)


def _attn_sample_call(z, cache_k, cache_v, bias_tab, q_norm_w, k_norm_w, heads):
    batch, seq, _ = z.shape
    n_cached = cache_k.shape[1]
    hd = HEAD_DIM
    wd = heads * hd
    q_pos = PAST_LEN + np.arange(seq)
    k_pos = PAST_LEN - n_cached + np.arange(n_cached + seq)
    qc, kc = q_pos // CHUNK, k_pos // CHUNK
    valid = (k_pos[None] >= 0) & (kc[None] <= qc[:, None]) & (kc[None] >= qc[:, None] - BAND_PREV)
    assert valid.all() and n_cached == PREV_ROWS and seq <= CHUNK and n_cached + seq <= BAND

    def zcol(seg):
        return pl.BlockSpec((1, seq, wd), lambda b: (b, 0, seg))

    cache_spec = pl.BlockSpec((1, n_cached, wd), lambda b: (b, 0, 0))
    kern = functools.partial(_attn_sample_kernel, seq=seq, n_cached=n_cached, heads=heads)
    return pl.pallas_call(
        kern,
        grid=(batch,),
        in_specs=[zcol(4), zcol(5), zcol(6), zcol(7),
                  cache_spec, cache_spec,
                  pl.BlockSpec((heads, PAIR_ROWS, PAIR_COLS), lambda b: (0, 0, 0)),
                  pl.BlockSpec((1, hd), lambda b: (0, 0)),
                  pl.BlockSpec((1, hd), lambda b: (0, 0))],
        out_specs=[pl.BlockSpec((1, seq, wd), lambda b: (b, 0, 0)),
                   pl.BlockSpec((1, seq, wd), lambda b: (b, 0, 0))],
        out_shape=[jax.ShapeDtypeStruct((batch, seq, wd), BF16),
                   jax.ShapeDtypeStruct((batch, seq, wd), F32)],
        compiler_params=_params("parallel"),
        name="band_attention_sample",
    )(z, z, z, z, cache_k.reshape(batch, n_cached, wd),
      cache_v.reshape(batch, n_cached, wd), bias_tab,
      q_norm_w.reshape(1, hd), k_norm_w.reshape(1, hd))


def _out_kernel(oa_ref, ob_ref, ma_ref, mb_ref, x_ref, g_ref, wa_ref, wb_ref, wo_ref, y_ref):
    bb, tt, d = x_ref.shape
    ya = _dot(oa_ref[...].reshape(bb * tt, -1), wa_ref[...])
    yb = _dot(ob_ref[...].reshape(bb * tt, -1), wb_ref[...])
    merged = (_sigmoid(ma_ref[...].reshape(bb * tt, d).astype(F32)) * ya
              + _sigmoid(mb_ref[...].reshape(bb * tt, d).astype(F32)) * yb)
    delta = _dot(merged.astype(BF16), wo_ref[...]).reshape(bb, tt, d)
    y_ref[...] = x_ref[...] + g_ref[...] * delta


def _out_call(o_a, o_b, z, x, gate, w_proj_a, w_proj_b, w_out, heads, layer):
    batch, seq, d = x.shape
    bb, tt = _row_tiling(batch, seq, 512)
    wa = o_a.shape[-1]
    wb = o_b.shape[-1]
    m_off = (8 * heads * HEAD_DIM) // d
    assert m_off * d == 8 * heads * HEAD_DIM

    def rows(width, off=0):
        return pl.BlockSpec((bb, tt, width), lambda b, i: (b, i, off))

    def whole(shape):
        return pl.BlockSpec((None,) + shape, lambda b, i: (layer, 0, 0), pipeline_mode=pl.Buffered(1))

    return pl.pallas_call(
        _out_kernel,
        grid=(batch // bb, seq // tt),
        in_specs=[rows(wa), rows(wb), rows(d, m_off), rows(d, m_off + 1), rows(d),
                  pl.BlockSpec((bb, 1, d), lambda b, i: (b, 0, 0)),
                  whole((wa, d)), whole((wb, d)), whole((d, d))],
        out_specs=rows(d),
        out_shape=jax.ShapeDtypeStruct((batch, seq, d), F32),
        compiler_params=_params("parallel", "parallel"),
        name="out_proj",
    )(o_a, o_b, z, z, x, gate, w_proj_a, w_proj_b, w_out)


def _layer(x, mod, conv_buf, s0, cache_k, cache_v, p, bias_tab, heads, prompt, layer):
    (norm_w, w_main, w_small, conv_w8, a_log, dt_bias, gdn_norm_w,
     q_norm_w, k_norm_w, w_proj_a, w_proj_b, w_out) = p
    batch, seq, d = x.shape
    hd = HEAD_DIM
    shift, scale, gate = [m.reshape(batch, 1, d) for m in jnp.split(mod, 3, axis=-1)]

    z, z_small = _inproj_call(x, scale, shift, norm_w, w_main, w_small, layer)

    logits = jnp.transpose(z_small[..., :2 * heads].reshape(batch, seq, 2, heads), (0, 3, 2, 1))
    rows = jnp.pad(logits, ((0, 0), (0, 0), (0, SUBLANES - 2), (0, 0)))
    conv_buf8 = jnp.pad(conv_buf, ((0, 0), (SUBLANES - (CONV_WIDTH - 1), 0), (0, 0)))
    o_a, s_new = _gdn_call(z, rows, conv_w8, conv_buf8, s0, a_log, dt_bias, gdn_norm_w, heads)

    conv_dim = 3 * heads * hd
    new_buf = z[:, seq - (CONV_WIDTH - 1):, :conv_dim].astype(F32)
    if prompt:
        o_b, k_new = _attn_prompt_call(z, bias_tab, q_norm_w, k_norm_w, heads)
        keep = k_new.shape[1]
        v_new = z[:, seq - keep:, 6 * heads * hd:7 * heads * hd]
    else:
        o_b, k_new = _attn_sample_call(z, cache_k, cache_v, bias_tab, q_norm_w, k_norm_w, heads)
        keep = seq
        v_new = z[:, :, 6 * heads * hd:7 * heads * hd]
    k_new = k_new.reshape(batch, keep, heads, hd)
    v_new = v_new.reshape(batch, keep, heads, hd).astype(F32)

    y = _out_call(o_a, o_b, z, x, gate, w_proj_a, w_proj_b, w_out, heads, layer)
    return y, new_buf, s_new, k_new, v_new


def kernel(x_prompt, x_sample, c_prompt, c_sample, state_conv, state_delta, cache_k, cache_v, norm_w, ada_w, ada_b, w_in, conv_w, gdn_a_log, gdn_dt_bias, gdn_norm_w, q_norm_w, k_norm_w, rel_bias, w_proj_a, w_proj_b, w_out):
    bp, _, d = x_prompt.shape
    depth = w_in.shape[0]
    heads = gdn_a_log.shape[1]
    hd = HEAD_DIM
    small0 = 4 * heads * hd
    assert w_in.shape[2] == 8 * heads * hd + 2 * heads + 2 * d
    assert conv_w.shape[1:] == (CONV_WIDTH, 3 * heads * hd) and rel_bias.shape[1:] == (heads, N_REL)

    mod = _mod_call(jnp.concatenate([c_prompt, c_sample], axis=0), ada_w, ada_b)
    bias_tab = _bias_call(rel_bias)

    w16 = w_in.astype(BF16)
    w_main = jnp.concatenate([w16[:, :, :small0], w16[:, :, small0 + 2 * heads:]], axis=2)
    w_small = jnp.pad(w16[:, :, small0:small0 + 2 * heads], ((0, 0), (0, 0), (0, LANES - 2 * heads)))
    w_proj_a, w_proj_b, w_out = (w.astype(BF16) for w in (w_proj_a, w_proj_b, w_out))

    yp, ys = x_prompt, x_sample
    outs_p, outs_s = [], []
    zero_buf = jnp.zeros((bp, CONV_WIDTH - 1, 3 * heads * hd), x_prompt.dtype)
    zero_state = jnp.zeros((bp, heads, hd, hd), state_delta.dtype)
    for l in range(depth):
        conv_w8 = jnp.pad(conv_w[l], ((0, SUBLANES - CONV_WIDTH), (0, 0)))
        p = (norm_w[l], w_main, w_small, conv_w8, gdn_a_log[l], gdn_dt_bias[l], gdn_norm_w[l],
             q_norm_w[l], k_norm_w[l], w_proj_a, w_proj_b, w_out)
        yp, *rest = _layer(yp, mod[l, :bp], zero_buf, zero_state, None, None, p, bias_tab[l], heads, True, l)
        outs_p.append(rest)
        ys, *rest = _layer(ys, mod[l, bp:], state_conv[l], state_delta[l], cache_k[l], cache_v[l],
                           p, bias_tab[l], heads, False, l)
        outs_s.append(rest)

    stack = lambda outs, i: jnp.stack([o[i] for o in outs])
    return (yp, ys,
            stack(outs_p, 0), stack(outs_p, 1), stack(outs_p, 2), stack(outs_p, 3),
            stack(outs_s, 0), stack(outs_s, 1), stack(outs_s, 2), stack(outs_s, 3))
```

```python
import functools
import math

import numpy as np
import jax
import jax.numpy as jnp
from jax import lax
from jax.experimental import pallas as pl
from jax.experimental.pallas import tpu as pltpu

F32 = jnp.float32
BF16 = jnp.bfloat16

EPS = 1e-6
CHUNK = 64
BAND_PREV = 8
REL_CLIP = 128
N_REL = 2 * REL_CLIP + 1
PAST_LEN = 2048
CONV_WIDTH = 4
HEAD_DIM = 128
LANES = 128
SUBLANES = 8
VMEM_LIMIT_BYTES = 56 * 1024 * 1024
GDN_BLOCK = 64
GDN_SUPER = 256
GDN_HEADS_PER_STEP = 8
CONV_PAD = 2 * SUBLANES
PREV_ROWS = BAND_PREV * CHUNK
BAND = PREV_ROWS + CHUNK
PAIR_ROWS = 2 * CHUNK
PAIR_COLS = PREV_ROWS + PAIR_ROWS
MASKED = -1e30
ATT_HEADS_PER_STEP = 4


def _dot(a, b):
    return jnp.dot(a, b, preferred_element_type=F32)


def _dot_nt(a, b):
    return lax.dot_general(a, b, (((1,), (1,)), ((), ())), preferred_element_type=F32)


def _dot_tn(a, b):
    return lax.dot_general(a, b, (((0,), (0,)), ((), ())), preferred_element_type=F32)


def _sigmoid(x):
    return 0.5 + 0.5 * jnp.tanh(0.5 * x)


def _silu(x):
    half = 0.5 * x
    return half + half * jnp.tanh(half)


def _rms(x, w):
    return x * lax.rsqrt(jnp.mean(x * x, axis=-1, keepdims=True) + EPS) * w


def _params(*sem):
    return pltpu.CompilerParams(dimension_semantics=sem, vmem_limit_bytes=VMEM_LIMIT_BYTES)


def _largest_divisor(n, candidates):
    for c in candidates:
        if n % c == 0:
            return c
    raise ValueError(f"no tile in {candidates} divides {n}")


def _row_tiling(batch, seq, rows):
    if seq >= rows:
        assert seq % rows == 0
        return 1, rows
    bb = min(batch, rows // seq)
    assert batch % bb == 0
    return bb, seq


def _mod_kernel(c_ref, w0_ref, w1_ref, w2_ref, b_ref, o_ref):
    @pl.when(pl.program_id(1) == 0)
    def _():
        o_ref[0] = jnp.broadcast_to(b_ref[0], o_ref.shape[1:])

    act = _silu(c_ref[...]).astype(BF16)
    n3 = w0_ref.shape[-1]
    for t, w_ref in enumerate((w0_ref, w1_ref, w2_ref)):
        o_ref[0, :, t * n3:(t + 1) * n3] += _dot(act, w_ref[0].astype(BF16))


def _mod_call(c, ada_w, ada_b):
    depth, d, n = ada_w.shape
    rows = c.shape[0]
    tk = _largest_divisor(d, (256, 128))
    n3 = n // 3
    assert n3 * 3 == n and n3 % LANES == 0

    def third(t):
        return pl.BlockSpec((1, tk, n3), lambda l, k: (l, k, t))

    return pl.pallas_call(
        _mod_kernel,
        grid=(depth, d // tk),
        in_specs=[pl.BlockSpec((rows, tk), lambda l, k: (0, k)),
                  third(0), third(1), third(2),
                  pl.BlockSpec((1, 1, n), lambda l, k: (l, 0, 0))],
        out_specs=pl.BlockSpec((1, rows, n), lambda l, k: (l, 0, 0)),
        out_shape=jax.ShapeDtypeStruct((depth, rows, n), F32),
        compiler_params=_params("parallel", "arbitrary"),
        name="adaln_mod",
    )(c, ada_w, ada_w, ada_w, ada_b.reshape(depth, 1, n))


def _bias_kernel(rb_ref, o_ref):
    rb = rb_ref[0]
    heads, nrel = rb.shape
    rb_hi = rb.astype(BF16)
    rest = rb - rb_hi.astype(F32)
    rb_mid = rest.astype(BF16)
    rb_lo = (rest - rb_mid.astype(F32)).astype(BF16)
    r = lax.broadcasted_iota(jnp.int32, (nrel, PAIR_COLS), 0)
    j = lax.broadcasted_iota(jnp.int32, (nrel, PAIR_COLS), 1)
    onehot = (jnp.clip(PREV_ROWS - j, -REL_CLIP, REL_CLIP) + REL_CLIP == r).astype(BF16)
    row0 = (_dot(rb_hi, onehot) + _dot(rb_mid, onehot)) + _dot(rb_lo, onehot)
    lane = lax.broadcasted_iota(jnp.int32, (heads, PAIR_COLS), 1)
    key_chunk = lane // CHUNK
    assert PREV_ROWS >= REL_CLIP
    far = rb[:, N_REL - 1:N_REL]

    def body(i, row):
        q_chunk = i // CHUNK
        in_band = (key_chunk >= q_chunk) & (key_chunk <= q_chunk + BAND_PREV)
        o_ref[0, i] = jnp.where(in_band, row, MASKED)
        return jnp.where(lane == 0, far, pltpu.roll(row, 1, axis=1))

    lax.fori_loop(0, PAIR_ROWS, body, row0)


def _bias_call(rel_bias):
    depth, heads, nrel = rel_bias.shape
    nrel_pad = -(-nrel // SUBLANES) * SUBLANES
    rb = jnp.pad(rel_bias, ((0, 0), (0, 0), (0, nrel_pad - nrel)))
    tab = pl.pallas_call(
        _bias_kernel,
        grid=(depth,),
        in_specs=[pl.BlockSpec((1, heads, nrel_pad), lambda l: (l, 0, 0))],
        out_specs=pl.BlockSpec((1, PAIR_ROWS, heads, PAIR_COLS), lambda l: (l, 0, 0, 0)),
        out_shape=jax.ShapeDtypeStruct((depth, PAIR_ROWS, heads, PAIR_COLS), F32),
        compiler_params=_params("parallel"),
        name="rel_bias_table",
    )(rb)
    return jnp.transpose(tab, (0, 2, 1, 3))


def _inproj_kernel(x_ref, sc_ref, sh_ref, nw_ref, w_ref, ws_ref, z_ref, zs_ref, h_ref, *, bb, tt, rc):
    @pl.when(pl.program_id(2) == 0)
    def _():
        for b in range(bb):
            gain = nw_ref[...] * (1.0 + sc_ref[b])
            for r0 in range(0, tt, rc):
                x = x_ref[b, r0:r0 + rc, :]
                inv = lax.rsqrt(jnp.mean(x * x, axis=-1, keepdims=True) + EPS)
                h_ref[b * tt + r0:b * tt + r0 + rc, :] = ((x * inv) * gain + sh_ref[b]).astype(BF16)
        zs_ref[...] = _dot(h_ref[...], ws_ref[...]).reshape(zs_ref.shape)

    z_ref[...] = _dot(h_ref[...], w_ref[...]).astype(z_ref.dtype).reshape(z_ref.shape)


def _inproj_call(x, scale, shift, norm_w, w_main, w_small, layer):
    batch, seq, d = x.shape
    n = w_main.shape[2]
    ns = w_small.shape[2]
    bb, tt = _row_tiling(batch, seq, 1024)
    tn = _largest_divisor(n, (2048, 1024, 512, 256, 128))
    kern = functools.partial(_inproj_kernel, bb=bb, tt=tt, rc=min(tt, 256))
    return pl.pallas_call(
        kern,
        grid=(batch // bb, seq // tt, n // tn),
        in_specs=[pl.BlockSpec((bb, tt, d), lambda b, i, j: (b, i, 0)),
                  pl.BlockSpec((bb, 1, d), lambda b, i, j: (b, 0, 0)),
                  pl.BlockSpec((bb, 1, d), lambda b, i, j: (b, 0, 0)),
                  pl.BlockSpec((1, d), lambda b, i, j: (0, 0)),
                  pl.BlockSpec((None, d, tn), lambda b, i, j: (layer, 0, j)),
                  pl.BlockSpec((None, d, ns), lambda b, i, j: (layer, 0, 0))],
        out_specs=[pl.BlockSpec((bb, tt, tn), lambda b, i, j: (b, i, j)),
                   pl.BlockSpec((bb, tt, ns), lambda b, i, j: (b, i, 0))],
        out_shape=[jax.ShapeDtypeStruct((batch, seq, n), BF16),
                   jax.ShapeDtypeStruct((batch, seq, ns), F32)],
        scratch_shapes=[pltpu.VMEM((bb * tt, d), BF16)],
        compiler_params=_params("parallel", "parallel", "arbitrary"),
        name="in_proj",
    )(x, scale, shift, norm_w.reshape(1, d), w_main, w_small)


def _gdn_kernel(alog_ref, dt_ref, q_ref, k_ref, v_ref, gate_ref, r_ref,
                cwq_ref, cwk_ref, cwv_ref, csq_ref, csk_ref, csv_ref, s0_ref, nw_ref,
                o_ref, sfin_ref, xs_ref, s_ref, *, tb, blk, hg):
    head0 = pl.program_id(1) * hg
    t = pl.program_id(2)
    pad = CONV_PAD
    hd = HEAD_DIM
    heads = range(hg)
    assert CONV_WIDTH == 4

    @pl.when(t == 0)
    def _():
        xs_ref[0, 0:pad, :] = csq_ref[0]
        xs_ref[1, 0:pad, :] = csk_ref[0]
        xs_ref[2, 0:pad, :] = csv_ref[0]
        s_ref[...] = s0_ref[0]

    xs_ref[0, pad:pad + tb, :] = q_ref[0].astype(F32)
    xs_ref[1, pad:pad + tb, :] = k_ref[0].astype(F32)
    xs_ref[2, pad:pad + tb, :] = v_ref[0].astype(F32)

    def conv_act(i, cw_ref, h):
        cols = slice(h * hd, (h + 1) * hd)
        w = cw_ref[:, cols]
        lead = SUBLANES
        x0 = xs_ref[i, pad - lead:pad + tb, cols]
        x1 = xs_ref[i, pad - lead - 1:pad + tb - 1, cols]
        late = x0[lead:] * w[3:4] + x1[lead:] * w[2:3]
        early = x0 * w[1:2] + x1 * w[0:1]
        return _silu(late + early[lead - 2:lead - 2 + tb])

    lt = min(LANES, tb)
    nblk = tb // blk

    def slab_of(mat, b):
        c0 = (b * blk) // lt * lt
        return mat[b * blk:(b + 1) * blk, c0:c0 + lt]

    def full_of(slabs):
        rows = []
        for b, sl in enumerate(slabs):
            tile = (b * blk) // lt
            parts = [sl if c == tile else jnp.zeros((blk, lt), sl.dtype) for c in range(tb // lt)]
            rows.append(parts[0] if len(parts) == 1 else jnp.concatenate(parts, axis=1))
        return rows[0] if len(rows) == 1 else jnp.concatenate(rows, axis=0)

    incl, strict, eye = [], [], []
    for b in range(nblk):
        ri = lax.broadcasted_iota(jnp.int32, (blk, lt), 0) + b * blk
        ci = lax.broadcasted_iota(jnp.int32, (blk, lt), 1) + (b * blk) // lt * lt
        same = (ci >= b * blk) & (ci < (b + 1) * blk)
        incl.append(same & (ri >= ci))
        strict.append(same & (ri > ci))
        eye.append((ri == ci).astype(F32))

    ri = lax.broadcasted_iota(jnp.int32, (tb, tb), 0)
    ci = lax.broadcasted_iota(jnp.int32, (tb, tb), 1)
    shift = int(math.log2(blk))
    same = (ri >> shift) == (ci >> shift)
    sum_mats = jnp.concatenate([(same & (ri <= ci)).astype(BF16), same.astype(BF16)], axis=1)
    row_id = lax.broadcasted_iota(jnp.int32, (SUBLANES, tb), 0)

    def split16(x):
        hi = x.astype(BF16)
        return hi, (x - hi.astype(F32)).astype(BF16)

    r = r_ref[0].reshape(hg * SUBLANES, tb)
    dt_rows = jnp.concatenate([jnp.full((SUBLANES, tb), dt_ref[head0 + h], F32) for h in heads], axis=0)
    alog_rows = jnp.concatenate([jnp.full((SUBLANES, tb), alog_ref[head0 + h], F32) for h in heads], axis=0)
    beta_rows = _sigmoid(r)
    xx = r + dt_rows
    softplus = jnp.maximum(xx, 0.0) + jnp.log1p(jnp.exp(-jnp.abs(xx)))
    g_rows = -jnp.exp(alog_rows) * softplus
    g_hi = g_rows.astype(BF16)
    g_rest = g_rows - g_hi.astype(F32)
    g_mid = g_rest.astype(BF16)
    g_lo = (g_rest - g_mid.astype(F32)).astype(BF16)
    sums = (_dot(g_hi, sum_mats) + _dot(g_mid, sum_mats)) + _dot(g_lo, sum_mats)

    q, k, v, neg_m, a_intra, beta_c, g_c, g_last_c = [], [], [], [], [], [], [], []
    for h in heads:
        aq = conv_act(0, cwq_ref, h)
        ak = conv_act(1, cwk_ref, h)
        v.append(conv_act(2, cwv_ref, h))
        q.append(aq * (lax.rsqrt(jnp.sum(aq * aq, axis=-1, keepdims=True) + EPS) * (hd ** -0.5)))
        k.append(ak * lax.rsqrt(jnp.sum(ak * ak, axis=-1, keepdims=True) + EPS))

        hs = slice(h * SUBLANES, (h + 1) * SUBLANES)
        g_cum_rows = sums[hs, :tb]
        cols = jnp.transpose(jnp.where(row_id == 0, beta_rows[hs], g_cum_rows))
        beta_c.append(cols[:, 0:1])
        g_c.append(cols[:, 1:2])
        g_last_c.append(jnp.transpose(sums[hs, tb:])[:, 1:2])
        g_r = g_cum_rows[1:2, :]

        k16 = k[h].astype(BF16)
        kk = _dot_nt(k16, k16)
        qk = _dot_nt(q[h].astype(BF16), k16)
        neg_m_h, a_h = [], []
        for b in range(nblk):
            rs = slice(b * blk, (b + 1) * blk)
            c0 = (b * blk) // lt * lt
            diff = g_c[h][rs] - g_r[:, c0:c0 + lt]
            decay = jnp.where(incl[b], jnp.exp(jnp.where(incl[b], diff, 0.0)), 0.0)
            neg_m_h.append(jnp.where(strict[b], -(beta_c[h][rs] * slab_of(kk, b)) * decay, 0.0))
            a_h.append((slab_of(qk, b) * decay).astype(BF16))
        neg_m.append(neg_m_h)
        a_intra.append(full_of(a_h))

    for i in range(3):
        xs_ref[i, 0:pad, :] = xs_ref[i, tb:tb + pad, :]

    n_hi, n_lo = [], []
    for h in heads:
        pieces = [split16(m) for m in neg_m[h]]
        n_hi.append(full_of([p[0] for p in pieces]))
        n_lo.append(full_of([p[1] for p in pieces]))
    x = [[eye[b] + neg_m[h][b] for b in range(nblk)] for h in heads]
    steps = shift - 1
    for step in range(steps):
        if step < steps - 1:
            x16 = [full_of([xb.astype(BF16) for xb in x[h]]) for h in heads]
            nx = [_dot(n_hi[h], x16[h]) for h in heads]
        else:
            pieces = [[split16(xb) for xb in x[h]] for h in heads]
            x16 = [full_of([p[0] for p in pieces[h]]) for h in heads]
            x_lo = [full_of([p[1] for p in pieces[h]]) for h in heads]
            nx = [_dot(n_hi[h], x16[h]) + (_dot(n_hi[h], x_lo[h]) + _dot(n_lo[h], x16[h])) for h in heads]
        resid = [full_of([((eye[b] - x[h][b]) + slab_of(nx[h], b)).astype(BF16) for b in range(nblk)])
                 for h in heads]
        upd = [_dot(x16[h], resid[h]) for h in heads]
        x = [[x[h][b] + slab_of(upd[h], b) for b in range(nblk)] for h in heads]
    x = [full_of([xb.astype(BF16) for xb in x[h]]) for h in heads]

    u, w, qg, kg, e_last = [], [], [], [], []
    for h in heads:
        e_g = jnp.exp(g_c[h])
        rhs = jnp.concatenate([v[h] * beta_c[h], k[h] * (beta_c[h] * e_g)], axis=1).astype(BF16)
        uw = _dot(x[h], rhs)
        u.append(uw[:, :hd])
        w.append(uw[:, hd:])
        qg.append(q[h] * e_g)
        kg.append((k[h] * jnp.exp(g_last_c[h] - g_c[h])).astype(BF16))
        e_last.append(jnp.exp(g_last_c[h]))

    s = [s_ref[h] for h in heads]
    v_new = [[] for _ in heads]
    o_inter = [[] for _ in heads]
    for b in range(tb // blk):
        rs = slice(b * blk, (b + 1) * blk)
        for h in heads:
            ws_qs = _dot(jnp.concatenate([w[h][rs], qg[h][rs]], axis=0).astype(BF16), s[h].astype(BF16))
            vn = u[h][rs] - ws_qs[:blk]
            v_new[h].append(vn)
            o_inter[h].append(ws_qs[blk:])
            s[h] = s[h] * e_last[h][b * blk:b * blk + 1, :] + _dot_tn(kg[h][rs], vn.astype(BF16))

    for h in heads:
        s_ref[h] = s[h]
        cols = slice(h * hd, (h + 1) * hd)
        vn = jnp.concatenate(v_new[h], axis=0).astype(BF16)
        o = jnp.concatenate(o_inter[h], axis=0) + _dot(a_intra[h], vn)
        o_ref[0, :, cols] = (_rms(o, nw_ref[...]) * _silu(gate_ref[0, :, cols].astype(F32))).astype(BF16)

    @pl.when(t == pl.num_programs(2) - 1)
    def _():
        sfin_ref[0] = s_ref[...]


def _gdn_call(z, rows, conv_w8, conv_buf8, s0, a_log, dt_bias, norm_w, heads):
    batch, seq, _ = z.shape
    tb = min(seq, GDN_SUPER)
    blk = min(tb, GDN_BLOCK)
    hg = min(heads, GDN_HEADS_PER_STEP)
    assert seq % tb == 0 and tb % blk == 0 and heads % hg == 0
    hd = HEAD_DIM
    wd = hg * hd
    ng = heads // hg
    smem = pl.BlockSpec(memory_space=pltpu.SMEM)

    def zcol(seg):
        return pl.BlockSpec((1, tb, wd), lambda b, g, t: (b, t, seg * ng + g))

    def cw(seg):
        return pl.BlockSpec((SUBLANES, wd), lambda b, g, t: (0, seg * ng + g))

    def cs(seg):
        return pl.BlockSpec((1, CONV_PAD, wd), lambda b, g, t: (b, 0, seg * ng + g))

    state_spec = pl.BlockSpec((1, hg, hd, hd), lambda b, g, t: (b, g, 0, 0))
    kern = functools.partial(_gdn_kernel, tb=tb, blk=blk, hg=hg)
    return pl.pallas_call(
        kern,
        grid=(batch, ng, seq // tb),
        in_specs=[smem, smem,
                  zcol(0), zcol(1), zcol(2), zcol(3),
                  pl.BlockSpec((1, hg, SUBLANES, tb), lambda b, g, t: (b, g, 0, t)),
                  cw(0), cw(1), cw(2),
                  cs(0), cs(1), cs(2),
                  state_spec,
                  pl.BlockSpec((1, hd), lambda b, g, t: (0, 0))],
        out_specs=[pl.BlockSpec((1, tb, wd), lambda b, g, t: (b, t, g)),
                   state_spec],
        out_shape=[jax.ShapeDtypeStruct((batch, seq, heads * hd), BF16),
                   jax.ShapeDtypeStruct((batch, heads, hd, hd), F32)],
        scratch_shapes=[pltpu.VMEM((3, tb + CONV_PAD, wd), F32),
                        pltpu.VMEM((hg, hd, hd), F32)],
        compiler_params=_params("parallel", "parallel", "arbitrary"),
        name="gated_deltanet",
    )(a_log, dt_bias, z, z, z, z, rows, conv_w8, conv_w8, conv_w8,
      conv_buf8, conv_buf8, conv_buf8, s0, norm_w.reshape(1, hd))


def _attn_prompt_kernel(q_ref, k_ref, v_ref, gate_ref, bias_ref, qw_ref, kw_ref,
                        o_ref, kc_ref, kbuf, vbuf, qs, bias_s, *, tq, hg):
    i = pl.program_id(2)
    hd = HEAD_DIM
    npairs = tq // PAIR_ROWS
    problems = [(h, p) for h in range(hg) for p in range(npairs)]

    for h in range(hg):
        cols = slice(h * hd, (h + 1) * hd)

        @pl.when(i == 0)
        def _():
            kbuf[h, 0:PREV_ROWS, :] = jnp.zeros((PREV_ROWS, hd), BF16)
            vbuf[h, 0:PREV_ROWS, :] = jnp.zeros((PREV_ROWS, hd), BF16)

        @pl.when(i > 0)
        def _():
            kbuf[h, 0:PREV_ROWS, :] = kbuf[h, tq:tq + PREV_ROWS, :]
            vbuf[h, 0:PREV_ROWS, :] = vbuf[h, tq:tq + PREV_ROWS, :]

        kn = _rms(k_ref[0, :, cols].astype(F32), kw_ref[...])
        kc_ref[0, :, cols] = kn
        kbuf[h, PREV_ROWS:PREV_ROWS + tq, :] = kn.astype(BF16)
        vbuf[h, PREV_ROWS:PREV_ROWS + tq, :] = v_ref[0, :, cols]
        qs[h] = (_rms(q_ref[0, :, cols].astype(F32), qw_ref[...]) * (hd ** -0.5)).astype(BF16)

    @pl.when(i == 0)
    def _():
        col = lax.broadcasted_iota(jnp.int32, (PAIR_ROWS, PAIR_COLS), 1)
        for h, p in problems:
            bias_s[h, p] = jnp.where(col < PREV_ROWS - p * PAIR_ROWS, MASKED, bias_ref[h])

    @pl.when(i == 1)
    def _():
        for h, p in problems:
            bias_s[h, p] = bias_ref[h]

    def window(p):
        return slice(p * PAIR_ROWS, p * PAIR_ROWS + PAIR_COLS)

    def rows(p):
        return slice(p * PAIR_ROWS, (p + 1) * PAIR_ROWS)

    s = [_dot_nt(qs[h, rows(p), :], kbuf[h, window(p), :]) + bias_s[h, p] for h, p in problems]
    e = [jnp.exp(x - jnp.max(x, axis=-1, keepdims=True)) for x in s]
    denom = [jnp.sum(x, axis=-1, keepdims=True) for x in e]
    for n, (h, p) in enumerate(problems):
        cols = slice(h * hd, (h + 1) * hd)
        o = _dot(e[n].astype(BF16), vbuf[h, window(p), :]) / denom[n]
        o_ref[0, rows(p), cols] = (o * _silu(gate_ref[0, rows(p), cols].astype(F32))).astype(BF16)


def _attn_prompt_call(z, bias_tab, q_norm_w, k_norm_w, heads):
    batch, seq, _ = z.shape
    tq = PREV_ROWS
    hg = min(heads, ATT_HEADS_PER_STEP)
    assert seq % tq == 0 and tq % PAIR_ROWS == 0 and heads % hg == 0
    hd = HEAD_DIM
    wd = hg * hd
    ng = heads // hg

    def zcol(seg):
        return pl.BlockSpec((1, tq, wd), lambda b, g, i: (b, i, seg * ng + g))

    kern = functools.partial(_attn_prompt_kernel, tq=tq, hg=hg)
    return pl.pallas_call(
        kern,
        grid=(batch, ng, seq // tq),
        in_specs=[zcol(4), zcol(5), zcol(6), zcol(7),
                  pl.BlockSpec((hg, PAIR_ROWS, PAIR_COLS), lambda b, g, i: (g, 0, 0)),
                  pl.BlockSpec((1, hd), lambda b, g, i: (0, 0)),
                  pl.BlockSpec((1, hd), lambda b, g, i: (0, 0))],
        out_specs=[pl.BlockSpec((1, tq, wd), lambda b, g, i: (b, i, g)),
                   pl.BlockSpec((1, tq, wd), lambda b, g, i: (b, 0, g))],
        out_shape=[jax.ShapeDtypeStruct((batch, seq, heads * hd), BF16),
                   jax.ShapeDtypeStruct((batch, tq, heads * hd), F32)],
        scratch_shapes=[pltpu.VMEM((hg, PREV_ROWS + tq, hd), BF16),
                        pltpu.VMEM((hg, PREV_ROWS + tq, hd), BF16),
                        pltpu.VMEM((hg, tq, hd), BF16),
                        pltpu.VMEM((hg, tq // PAIR_ROWS, PAIR_ROWS, PAIR_COLS), F32)],
        compiler_params=_params("parallel", "parallel", "arbitrary"),
        name="band_attention_prompt",
    )(z, z, z, z, bias_tab, q_norm_w.reshape(1, hd), k_norm_w.reshape(1, hd))


def _attn_sample_kernel(q_ref, k_ref, v_ref, gate_ref, ck_ref, cv_ref, bias_ref, qw_ref, kw_ref,
                        o_ref, kn_ref, *, seq, n_cached, heads):
    hd = HEAD_DIM
    for h in range(heads):
        cols = slice(h * hd, (h + 1) * hd)
        kn = _rms(k_ref[0, :, cols].astype(F32), kw_ref[...])
        kn_ref[0, :, cols] = kn
        q16 = (_rms(q_ref[0, :, cols].astype(F32), qw_ref[...]) * (hd ** -0.5)).astype(BF16)
        bias = bias_ref[h]
        s_old = _dot_nt(q16, ck_ref[0, :, cols].astype(BF16)) + bias[:seq, :n_cached]
        s_new = _dot_nt(q16, kn.astype(BF16)) + bias[:seq, n_cached:n_cached + seq]
        m = jnp.maximum(jnp.max(s_old, axis=-1, keepdims=True), jnp.max(s_new, axis=-1, keepdims=True))
        p_old = jnp.exp(s_old - m)
        p_new = jnp.exp(s_new - m)
        denom = jnp.sum(p_old, axis=-1, keepdims=True) + jnp.sum(p_new, axis=-1, keepdims=True)
        o = (_dot(p_old.astype(BF16), cv_ref[0, :, cols].astype(BF16))
             + _dot(p_new.astype(BF16), v_ref[0, :, cols])) / denom
        o_ref[0, :, cols] = (o * _silu(gate_ref[0, :, cols].astype(F32))).astype(BF16)


def _attn_sample_call(z, cache_k, cache_v, bias_tab, q_norm_w, k_norm_w, heads):
    batch, seq, _ = z.shape
    n_cached = cache_k.shape[1]
    hd = HEAD_DIM
    wd = heads * hd
    q_pos = PAST_LEN + np.arange(seq)
    k_pos = PAST_LEN - n_cached + np.arange(n_cached + seq)
    qc, kc = q_pos // CHUNK, k_pos // CHUNK
    valid = (k_pos[None] >= 0) & (kc[None] <= qc[:, None]) & (kc[None] >= qc[:, None] - BAND_PREV)
    assert valid.all() and n_cached == PREV_ROWS and seq <= CHUNK and n_cached + seq <= BAND

    def zcol(seg):
        return pl.BlockSpec((1, seq, wd), lambda b: (b, 0, seg))

    cache_spec = pl.BlockSpec((1, n_cached, wd), lambda b: (b, 0, 0))
    kern = functools.partial(_attn_sample_kernel, seq=seq, n_cached=n_cached, heads=heads)
    return pl.pallas_call(
        kern,
        grid=(batch,),
        in_specs=[zcol(4), zcol(5), zcol(6), zcol(7),
                  cache_spec, cache_spec,
                  pl.BlockSpec((heads, PAIR_ROWS, PAIR_COLS), lambda b: (0, 0, 0)),
                  pl.BlockSpec((1, hd), lambda b: (0, 0)),
                  pl.BlockSpec((1, hd), lambda b: (0, 0))],
        out_specs=[pl.BlockSpec((1, seq, wd), lambda b: (b, 0, 0)),
                   pl.BlockSpec((1, seq, wd), lambda b: (b, 0, 0))],
        out_shape=[jax.ShapeDtypeStruct((batch, seq, wd), BF16),
                   jax.ShapeDtypeStruct((batch, seq, wd), F32)],
        compiler_params=_params("parallel"),
        name="band_attention_sample",
    )(z, z, z, z, cache_k.reshape(batch, n_cached, wd),
      cache_v.reshape(batch, n_cached, wd), bias_tab,
      q_norm_w.reshape(1, hd), k_norm_w.reshape(1, hd))


def _out_kernel(oa_ref, ob_ref, ma_ref, mb_ref, x_ref, g_ref, wa_ref, wb_ref, wo_ref, y_ref):
    bb, tt, d = x_ref.shape
    ya = _dot(oa_ref[...].reshape(bb * tt, -1), wa_ref[...])
    yb = _dot(ob_ref[...].reshape(bb * tt, -1), wb_ref[...])
    merged = (_sigmoid(ma_ref[...].reshape(bb * tt, d).astype(F32)) * ya
              + _sigmoid(mb_ref[...].reshape(bb * tt, d).astype(F32)) * yb)
    delta = _dot(merged.astype(BF16), wo_ref[...]).reshape(bb, tt, d)
    y_ref[...] = x_ref[...] + g_ref[...] * delta


def _out_call(o_a, o_b, z, x, gate, w_proj_a, w_proj_b, w_out, heads, layer):
    batch, seq, d = x.shape
    bb, tt = _row_tiling(batch, seq, 512)
    wa = o_a.shape[-1]
    wb = o_b.shape[-1]
    m_off = (8 * heads * HEAD_DIM) // d
    assert m_off * d == 8 * heads * HEAD_DIM

    def rows(width, off=0):
        return pl.BlockSpec((bb, tt, width), lambda b, i: (b, i, off))

    def whole(shape):
        return pl.BlockSpec((None,) + shape, lambda b, i: (layer, 0, 0), pipeline_mode=pl.Buffered(1))

    return pl.pallas_call(
        _out_kernel,
        grid=(batch // bb, seq // tt),
        in_specs=[rows(wa), rows(wb), rows(d, m_off), rows(d, m_off + 1), rows(d),
                  pl.BlockSpec((bb, 1, d), lambda b, i: (b, 0, 0)),
                  whole((wa, d)), whole((wb, d)), whole((d, d))],
        out_specs=rows(d),
        out_shape=jax.ShapeDtypeStruct((batch, seq, d), F32),
        compiler_params=_params("parallel", "parallel"),
        name="out_proj",
    )(o_a, o_b, z, z, x, gate, w_proj_a, w_proj_b, w_out)


def _layer(x, mod, conv_buf, s0, cache_k, cache_v, p, bias_tab, heads, prompt, layer):
    (norm_w, w_main, w_small, conv_w8, a_log, dt_bias, gdn_norm_w,
     q_norm_w, k_norm_w, w_proj_a, w_proj_b, w_out) = p
    batch, seq, d = x.shape
    hd = HEAD_DIM
    shift, scale, gate = [m.reshape(batch, 1, d) for m in jnp.split(mod, 3, axis=-1)]

    z, z_small = _inproj_call(x, scale, shift, norm_w, w_main, w_small, layer)

    logits = jnp.transpose(z_small[..., :2 * heads].reshape(batch, seq, 2, heads), (0, 3, 2, 1))
    rows = jnp.pad(logits, ((0, 0), (0, 0), (0, SUBLANES - 2), (0, 0)))
    conv_buf8 = jnp.pad(conv_buf, ((0, 0), (CONV_PAD - (CONV_WIDTH - 1), 0), (0, 0)))
    o_a, s_new = _gdn_call(z, rows, conv_w8, conv_buf8, s0, a_log, dt_bias, gdn_norm_w, heads)

    conv_dim = 3 * heads * hd
    new_buf = z[:, seq - (CONV_WIDTH - 1):, :conv_dim].astype(F32)
    if prompt:
        o_b, k_new = _attn_prompt_call(z, bias_tab, q_norm_w, k_norm_w, heads)
        keep = k_new.shape[1]
        v_new = z[:, seq - keep:, 6 * heads * hd:7 * heads * hd]
    else:
        o_b, k_new = _attn_sample_call(z, cache_k, cache_v, bias_tab, q_norm_w, k_norm_w, heads)
        keep = seq
        v_new = z[:, :, 6 * heads * hd:7 * heads * hd]
    k_new = k_new.reshape(batch, keep, heads, hd)
    v_new = v_new.reshape(batch, keep, heads, hd).astype(F32)

    y = _out_call(o_a, o_b, z, x, gate, w_proj_a, w_proj_b, w_out, heads, layer)
    return y, new_buf, s_new, k_new, v_new


def kernel(x_prompt, x_sample, c_prompt, c_sample, state_conv, state_delta, cache_k, cache_v, norm_w, ada_w, ada_b, w_in, conv_w, gdn_a_log, gdn_dt_bias, gdn_norm_w, q_norm_w, k_norm_w, rel_bias, w_proj_a, w_proj_b, w_out):
    bp, _, d = x_prompt.shape
    depth = w_in.shape[0]
    heads = gdn_a_log.shape[1]
    hd = HEAD_DIM
    small0 = 4 * heads * hd
    assert w_in.shape[2] == 8 * heads * hd + 2 * heads + 2 * d
    assert conv_w.shape[1:] == (CONV_WIDTH, 3 * heads * hd) and rel_bias.shape[1:] == (heads, N_REL)

    mod = _mod_call(jnp.concatenate([c_prompt, c_sample], axis=0), ada_w, ada_b)
    bias_tab = _bias_call(rel_bias)

    w16 = w_in.astype(BF16)
    w_main = jnp.concatenate([w16[:, :, :small0], w16[:, :, small0 + 2 * heads:]], axis=2)
    w_small = jnp.pad(w16[:, :, small0:small0 + 2 * heads], ((0, 0), (0, 0), (0, LANES - 2 * heads)))
    w_proj_a, w_proj_b, w_out = (w.astype(BF16) for w in (w_proj_a, w_proj_b, w_out))

    yp, ys = x_prompt, x_sample
    outs_p, outs_s = [], []
    zero_buf = jnp.zeros((bp, CONV_WIDTH - 1, 3 * heads * hd), x_prompt.dtype)
    zero_state = jnp.zeros((bp, heads, hd, hd), state_delta.dtype)
    for l in range(depth):
        conv_w8 = jnp.pad(conv_w[l], ((0, SUBLANES - CONV_WIDTH), (0, 0)))
        p = (norm_w[l], w_main, w_small, conv_w8, gdn_a_log[l], gdn_dt_bias[l], gdn_norm_w[l],
             q_norm_w[l], k_norm_w[l], w_proj_a, w_proj_b, w_out)
        yp, *rest = _layer(yp, mod[l, :bp], zero_buf, zero_state, None, None, p, bias_tab[l], heads, True, l)
        outs_p.append(rest)
        ys, *rest = _layer(ys, mod[l, bp:], state_conv[l], state_delta[l], cache_k[l], cache_v[l],
                           p, bias_tab[l], heads, False, l)
        outs_s.append(rest)

    stack = lambda outs, i: jnp.stack([o[i] for o in outs])
    return (yp, ys,
            stack(outs_p, 0), stack(outs_p, 1), stack(outs_p, 2), stack(outs_p, 3),
            stack(outs_s, 0), stack(outs_s, 1), stack(outs_s, 2), stack(outs_s, 3))
```

```python
import functools
import math

import numpy as np
import jax
import jax.numpy as jnp
from jax import lax
from jax.experimental import pallas as pl
from jax.experimental.pallas import tpu as pltpu

F32 = jnp.float32
BF16 = jnp.bfloat16

EPS = 1e-6
CHUNK = 64
BAND_PREV = 8
REL_CLIP = 128
N_REL = 2 * REL_CLIP + 1
PAST_LEN = 2048
CONV_WIDTH = 4
HEAD_DIM = 128
LANES = 128
SUBLANES = 8
VMEM_LIMIT_BYTES = 56 * 1024 * 1024
GDN_BLOCK = 64
GDN_SUPER = 256
GDN_HEADS_PER_STEP = 8
CONV_PAD = 2 * SUBLANES
PREV_ROWS = BAND_PREV * CHUNK
BAND = PREV_ROWS + CHUNK
PAIR_ROWS = 2 * CHUNK
PAIR_COLS = PREV_ROWS + PAIR_ROWS
MASKED = -1e30
ATT_HEADS_PER_STEP = 4


def _dot(a, b):
    return jnp.dot(a, b, preferred_element_type=F32)


def _dot_nt(a, b):
    return lax.dot_general(a, b, (((1,), (1,)), ((), ())), preferred_element_type=F32)


def _dot_tn(a, b):
    return lax.dot_general(a, b, (((0,), (0,)), ((), ())), preferred_element_type=F32)


def _sigmoid(x):
    return 0.5 + 0.5 * jnp.tanh(0.5 * x)


def _silu(x):
    half = 0.5 * x
    return half + half * jnp.tanh(half)


def _rms(x, w):
    return x * lax.rsqrt(jnp.mean(x * x, axis=-1, keepdims=True) + EPS) * w


def _params(*sem):
    return pltpu.CompilerParams(dimension_semantics=sem, vmem_limit_bytes=VMEM_LIMIT_BYTES)


def _largest_divisor(n, candidates):
    for c in candidates:
        if n % c == 0:
            return c
    raise ValueError(f"no tile in {candidates} divides {n}")


def _row_tiling(batch, seq, rows):
    if seq >= rows:
        assert seq % rows == 0
        return 1, rows
    bb = min(batch, rows // seq)
    assert batch % bb == 0
    return bb, seq


def _mod_kernel(c_ref, w0_ref, w1_ref, w2_ref, b_ref, o_ref):
    @pl.when(pl.program_id(1) == 0)
    def _():
        o_ref[0] = jnp.broadcast_to(b_ref[0], o_ref.shape[1:])

    act = _silu(c_ref[...]).astype(BF16)
    n3 = w0_ref.shape[-1]
    for t, w_ref in enumerate((w0_ref, w1_ref, w2_ref)):
        o_ref[0, :, t * n3:(t + 1) * n3] += _dot(act, w_ref[0].astype(BF16))


def _mod_call(c, ada_w, ada_b):
    depth, d, n = ada_w.shape
    rows = c.shape[0]
    tk = _largest_divisor(d, (256, 128))
    n3 = n // 3
    assert n3 * 3 == n and n3 % LANES == 0

    def third(t):
        return pl.BlockSpec((1, tk, n3), lambda l, k: (l, k, t))

    return pl.pallas_call(
        _mod_kernel,
        grid=(depth, d // tk),
        in_specs=[pl.BlockSpec((rows, tk), lambda l, k: (0, k)),
                  third(0), third(1), third(2),
                  pl.BlockSpec((1, 1, n), lambda l, k: (l, 0, 0))],
        out_specs=pl.BlockSpec((1, rows, n), lambda l, k: (l, 0, 0)),
        out_shape=jax.ShapeDtypeStruct((depth, rows, n), F32),
        compiler_params=_params("parallel", "arbitrary"),
        name="adaln_mod",
    )(c, ada_w, ada_w, ada_w, ada_b.reshape(depth, 1, n))


def _bias_kernel(rb_ref, o_ref):
    rb = rb_ref[0]
    heads, nrel = rb.shape
    rb_hi = rb.astype(BF16)
    rest = rb - rb_hi.astype(F32)
    rb_mid = rest.astype(BF16)
    rb_lo = (rest - rb_mid.astype(F32)).astype(BF16)
    r = lax.broadcasted_iota(jnp.int32, (nrel, PAIR_COLS), 0)
    j = lax.broadcasted_iota(jnp.int32, (nrel, PAIR_COLS), 1)
    onehot = (jnp.clip(PREV_ROWS - j, -REL_CLIP, REL_CLIP) + REL_CLIP == r).astype(BF16)
    row0 = (_dot(rb_hi, onehot) + _dot(rb_mid, onehot)) + _dot(rb_lo, onehot)
    lane = lax.broadcasted_iota(jnp.int32, (heads, PAIR_COLS), 1)
    key_chunk = lane // CHUNK
    assert PREV_ROWS >= REL_CLIP
    far = rb[:, N_REL - 1:N_REL]

    def body(i, row):
        q_chunk = i // CHUNK
        in_band = (key_chunk >= q_chunk) & (key_chunk <= q_chunk + BAND_PREV)
        o_ref[0, i] = jnp.where(in_band, row, MASKED)
        return jnp.where(lane == 0, far, pltpu.roll(row, 1, axis=1))

    lax.fori_loop(0, PAIR_ROWS, body, row0)


def _bias_call(rel_bias):
    depth, heads, nrel = rel_bias.shape
    nrel_pad = -(-nrel // SUBLANES) * SUBLANES
    rb = jnp.pad(rel_bias, ((0, 0), (0, 0), (0, nrel_pad - nrel)))
    tab = pl.pallas_call(
        _bias_kernel,
        grid=(depth,),
        in_specs=[pl.BlockSpec((1, heads, nrel_pad), lambda l: (l, 0, 0))],
        out_specs=pl.BlockSpec((1, PAIR_ROWS, heads, PAIR_COLS), lambda l: (l, 0, 0, 0)),
        out_shape=jax.ShapeDtypeStruct((depth, PAIR_ROWS, heads, PAIR_COLS), F32),
        compiler_params=_params("parallel"),
        name="rel_bias_table",
    )(rb)
    return jnp.transpose(tab, (0, 2, 1, 3))


def _inproj_kernel(x_ref, sc_ref, sh_ref, nw_ref, w_ref, ws_ref, z_ref, zs_ref, h_ref, *, bb, tt, rc):
    @pl.when(pl.program_id(2) == 0)
    def _():
        for b in range(bb):
            gain = nw_ref[...] * (1.0 + sc_ref[b])
            for r0 in range(0, tt, rc):
                x = x_ref[b, r0:r0 + rc, :]
                inv = lax.rsqrt(jnp.mean(x * x, axis=-1, keepdims=True) + EPS)
                h_ref[b * tt + r0:b * tt + r0 + rc, :] = ((x * inv) * gain + sh_ref[b]).astype(BF16)
        zs_ref[...] = _dot(h_ref[...], ws_ref[...]).reshape(zs_ref.shape)

    z_ref[...] = _dot(h_ref[...], w_ref[...]).astype(z_ref.dtype).reshape(z_ref.shape)


def _inproj_call(x, scale, shift, norm_w, w_main, w_small, layer):
    batch, seq, d = x.shape
    n = w_main.shape[2]
    ns = w_small.shape[2]
    bb, tt = _row_tiling(batch, seq, 1024)
    tn = _largest_divisor(n, (2048, 1024, 512, 256, 128))
    kern = functools.partial(_inproj_kernel, bb=bb, tt=tt, rc=min(tt, 256))
    return pl.pallas_call(
        kern,
        grid=(batch // bb, seq // tt, n // tn),
        in_specs=[pl.BlockSpec((bb, tt, d), lambda b, i, j: (b, i, 0)),
                  pl.BlockSpec((bb, 1, d), lambda b, i, j: (b, 0, 0)),
                  pl.BlockSpec((bb, 1, d), lambda b, i, j: (b, 0, 0)),
                  pl.BlockSpec((1, d), lambda b, i, j: (0, 0)),
                  pl.BlockSpec((None, d, tn), lambda b, i, j: (layer, 0, j)),
                  pl.BlockSpec((None, d, ns), lambda b, i, j: (layer, 0, 0))],
        out_specs=[pl.BlockSpec((bb, tt, tn), lambda b, i, j: (b, i, j)),
                   pl.BlockSpec((bb, tt, ns), lambda b, i, j: (b, i, 0))],
        out_shape=[jax.ShapeDtypeStruct((batch, seq, n), BF16),
                   jax.ShapeDtypeStruct((batch, seq, ns), F32)],
        scratch_shapes=[pltpu.VMEM((bb * tt, d), BF16)],
        compiler_params=_params("parallel", "parallel", "arbitrary"),
        name="in_proj",
    )(x, scale, shift, norm_w.reshape(1, d), w_main, w_small)


def _gdn_kernel(alog_ref, dt_ref, q_ref, k_ref, v_ref, gate_ref, r_ref,
                cwq_ref, cwk_ref, cwv_ref, csq_ref, csk_ref, csv_ref, s0_ref, nw_ref,
                o_ref, sfin_ref, xs_ref, s_ref, *, tb, blk, hg):
    head0 = pl.program_id(1) * hg
    t = pl.program_id(2)
    pad = CONV_PAD
    hd = HEAD_DIM
    heads = range(hg)
    assert CONV_WIDTH == 4

    @pl.when(t == 0)
    def _():
        xs_ref[0, 0:pad, :] = csq_ref[0]
        xs_ref[1, 0:pad, :] = csk_ref[0]
        xs_ref[2, 0:pad, :] = csv_ref[0]
        s_ref[...] = s0_ref[0]

    xs_ref[0, pad:pad + tb, :] = q_ref[0].astype(F32)
    xs_ref[1, pad:pad + tb, :] = k_ref[0].astype(F32)
    xs_ref[2, pad:pad + tb, :] = v_ref[0].astype(F32)

    def conv_act(i, cw_ref, h):
        cols = slice(h * hd, (h + 1) * hd)
        w = cw_ref[:, cols]
        groups = (pad + tb) // SUBLANES
        sub = lax.broadcasted_iota(jnp.int32, (1, SUBLANES, hd), 1)

        def shifted(a, n):
            r = pltpu.roll(a, n, axis=1)
            return jnp.where(sub >= n, r[1:], r[:-1])

        x = xs_ref[i, :, cols].reshape(groups, SUBLANES, hd)
        x1 = shifted(x, 1)
        late = x[2:] * w[3:4] + x1[1:] * w[2:3]
        early = x[1:] * w[1:2] + x1 * w[0:1]
        return _silu(late + shifted(early, 2)).reshape(tb, hd)

    lt = min(LANES, tb)
    nblk = tb // blk

    def slab_of(mat, b):
        c0 = (b * blk) // lt * lt
        return mat[b * blk:(b + 1) * blk, c0:c0 + lt]

    def full_of(slabs):
        rows = []
        for b, sl in enumerate(slabs):
            tile = (b * blk) // lt
            parts = [sl if c == tile else jnp.zeros((blk, lt), sl.dtype) for c in range(tb // lt)]
            rows.append(parts[0] if len(parts) == 1 else jnp.concatenate(parts, axis=1))
        return rows[0] if len(rows) == 1 else jnp.concatenate(rows, axis=0)

    incl, strict, eye = [], [], []
    for b in range(nblk):
        ri = lax.broadcasted_iota(jnp.int32, (blk, lt), 0) + b * blk
        ci = lax.broadcasted_iota(jnp.int32, (blk, lt), 1) + (b * blk) // lt * lt
        same = (ci >= b * blk) & (ci < (b + 1) * blk)
        incl.append(same & (ri >= ci))
        strict.append(same & (ri > ci))
        eye.append((ri == ci).astype(F32))

    ri = lax.broadcasted_iota(jnp.int32, (tb, tb), 0)
    ci = lax.broadcasted_iota(jnp.int32, (tb, tb), 1)
    shift = int(math.log2(blk))
    same = (ri >> shift) == (ci >> shift)
    sum_mats = jnp.concatenate([(same & (ri <= ci)).astype(BF16), same.astype(BF16)], axis=1)
    row_id = lax.broadcasted_iota(jnp.int32, (SUBLANES, tb), 0)

    def split16(x):
        hi = x.astype(BF16)
        return hi, (x - hi.astype(F32)).astype(BF16)

    r = r_ref[0].reshape(hg * SUBLANES, tb)
    dt_rows = jnp.concatenate([jnp.full((SUBLANES, tb), dt_ref[head0 + h], F32) for h in heads], axis=0)
    alog_rows = jnp.concatenate([jnp.full((SUBLANES, tb), alog_ref[head0 + h], F32) for h in heads], axis=0)
    beta_rows = _sigmoid(r)
    xx = r + dt_rows
    softplus = jnp.maximum(xx, 0.0) + jnp.log1p(jnp.exp(-jnp.abs(xx)))
    g_rows = -jnp.exp(alog_rows) * softplus
    g_hi = g_rows.astype(BF16)
    g_rest = g_rows - g_hi.astype(F32)
    g_mid = g_rest.astype(BF16)
    g_lo = (g_rest - g_mid.astype(F32)).astype(BF16)
    sums = (_dot(g_hi, sum_mats) + _dot(g_mid, sum_mats)) + _dot(g_lo, sum_mats)

    q, k, v, neg_m, a_intra, beta_c, g_c, g_last_c = [], [], [], [], [], [], [], []
    for h in heads:
        aq = conv_act(0, cwq_ref, h)
        ak = conv_act(1, cwk_ref, h)
        v.append(conv_act(2, cwv_ref, h))
        q.append(aq * (lax.rsqrt(jnp.sum(aq * aq, axis=-1, keepdims=True) + EPS) * (hd ** -0.5)))
        k.append(ak * lax.rsqrt(jnp.sum(ak * ak, axis=-1, keepdims=True) + EPS))

        hs = slice(h * SUBLANES, (h + 1) * SUBLANES)
        g_cum_rows = sums[hs, :tb]
        cols = jnp.transpose(jnp.where(row_id == 0, beta_rows[hs], g_cum_rows))
        beta_c.append(cols[:, 0:1])
        g_c.append(cols[:, 1:2])
        g_last_c.append(jnp.transpose(sums[hs, tb:])[:, 1:2])
        g_r = g_cum_rows[1:2, :]

        k16 = k[h].astype(BF16)
        kk = _dot_nt(k16, k16)
        qk = _dot_nt(q[h].astype(BF16), k16)
        neg_m_h, a_h = [], []
        for b in range(nblk):
            rs = slice(b * blk, (b + 1) * blk)
            c0 = (b * blk) // lt * lt
            diff = g_c[h][rs] - g_r[:, c0:c0 + lt]
            decay = jnp.where(incl[b], jnp.exp(jnp.where(incl[b], diff, 0.0)), 0.0)
            neg_m_h.append(jnp.where(strict[b], -(beta_c[h][rs] * slab_of(kk, b)) * decay, 0.0))
            a_h.append((slab_of(qk, b) * decay).astype(BF16))
        neg_m.append(neg_m_h)
        a_intra.append(full_of(a_h))

    for i in range(3):
        xs_ref[i, 0:pad, :] = xs_ref[i, tb:tb + pad, :]

    n_hi, n_lo = [], []
    for h in heads:
        pieces = [split16(m) for m in neg_m[h]]
        n_hi.append(full_of([p[0] for p in pieces]))
        n_lo.append(full_of([p[1] for p in pieces]))
    x = [[eye[b] + neg_m[h][b] for b in range(nblk)] for h in heads]
    steps = shift - 1
    for step in range(steps):
        if step < steps - 1:
            x16 = [full_of([xb.astype(BF16) for xb in x[h]]) for h in heads]
            nx = [_dot(n_hi[h], x16[h]) for h in heads]
        else:
            pieces = [[split16(xb) for xb in x[h]] for h in heads]
            x16 = [full_of([p[0] for p in pieces[h]]) for h in heads]
            x_lo = [full_of([p[1] for p in pieces[h]]) for h in heads]
            nx = [_dot(n_hi[h], x16[h]) + (_dot(n_hi[h], x_lo[h]) + _dot(n_lo[h], x16[h])) for h in heads]
        resid = [full_of([((eye[b] - x[h][b]) + slab_of(nx[h], b)).astype(BF16) for b in range(nblk)])
                 for h in heads]
        upd = [_dot(x16[h], resid[h]) for h in heads]
        x = [[x[h][b] + slab_of(upd[h], b) for b in range(nblk)] for h in heads]
    x = [full_of([xb.astype(BF16) for xb in x[h]]) for h in heads]

    u, w, qg, kg, e_last = [], [], [], [], []
    for h in heads:
        e_g = jnp.exp(g_c[h])
        rhs = jnp.concatenate([v[h] * beta_c[h], k[h] * (beta_c[h] * e_g)], axis=1).astype(BF16)
        uw = _dot(x[h], rhs)
        u.append(uw[:, :hd])
        w.append(uw[:, hd:])
        qg.append(q[h] * e_g)
        kg.append((k[h] * jnp.exp(g_last_c[h] - g_c[h])).astype(BF16))
        e_last.append(jnp.exp(g_last_c[h]))

    s = [s_ref[h] for h in heads]
    v_new = [[] for _ in heads]
    o_inter = [[] for _ in heads]
    for b in range(tb // blk):
        rs = slice(b * blk, (b + 1) * blk)
        for h in heads:
            ws_qs = _dot(jnp.concatenate([w[h][rs], qg[h][rs]], axis=0).astype(BF16), s[h].astype(BF16))
            vn = u[h][rs] - ws_qs[:blk]
            v_new[h].append(vn)
            o_inter[h].append(ws_qs[blk:])
            s[h] = s[h] * e_last[h][b * blk:b * blk + 1, :] + _dot_tn(kg[h][rs], vn.astype(BF16))

    for h in heads:
        s_ref[h] = s[h]
        cols = slice(h * hd, (h + 1) * hd)
        vn = jnp.concatenate(v_new[h], axis=0).astype(BF16)
        o = jnp.concatenate(o_inter[h], axis=0) + _dot(a_intra[h], vn)
        o_ref[0, :, cols] = (_rms(o, nw_ref[...]) * _silu(gate_ref[0, :, cols].astype(F32))).astype(BF16)

    @pl.when(t == pl.num_programs(2) - 1)
    def _():
        sfin_ref[0] = s_ref[...]


def _gdn_call(z, rows, conv_w8, conv_buf8, s0, a_log, dt_bias, norm_w, heads):
    batch, seq, _ = z.shape
    tb = min(seq, GDN_SUPER)
    blk = min(tb, GDN_BLOCK)
    hg = min(heads, GDN_HEADS_PER_STEP)
    assert seq % tb == 0 and tb % blk == 0 and heads % hg == 0
    hd = HEAD_DIM
    wd = hg * hd
    ng = heads // hg
    smem = pl.BlockSpec(memory_space=pltpu.SMEM)

    def zcol(seg):
        return pl.BlockSpec((1, tb, wd), lambda b, g, t: (b, t, seg * ng + g))

    def cw(seg):
        return pl.BlockSpec((SUBLANES, wd), lambda b, g, t: (0, seg * ng + g))

    def cs(seg):
        return pl.BlockSpec((1, CONV_PAD, wd), lambda b, g, t: (b, 0, seg * ng + g))

    state_spec = pl.BlockSpec((1, hg, hd, hd), lambda b, g, t: (b, g, 0, 0))
    kern = functools.partial(_gdn_kernel, tb=tb, blk=blk, hg=hg)
    return pl.pallas_call(
        kern,
        grid=(batch, ng, seq // tb),
        in_specs=[smem, smem,
                  zcol(0), zcol(1), zcol(2), zcol(3),
                  pl.BlockSpec((1, hg, SUBLANES, tb), lambda b, g, t: (b, g, 0, t)),
                  cw(0), cw(1), cw(2),
                  cs(0), cs(1), cs(2),
                  state_spec,
                  pl.BlockSpec((1, hd), lambda b, g, t: (0, 0))],
        out_specs=[pl.BlockSpec((1, tb, wd), lambda b, g, t: (b, t, g)),
                   state_spec],
        out_shape=[jax.ShapeDtypeStruct((batch, seq, heads * hd), BF16),
                   jax.ShapeDtypeStruct((batch, heads, hd, hd), F32)],
        scratch_shapes=[pltpu.VMEM((3, tb + CONV_PAD, wd), F32),
                        pltpu.VMEM((hg, hd, hd), F32)],
        compiler_params=_params("parallel", "parallel", "arbitrary"),
        name="gated_deltanet",
    )(a_log, dt_bias, z, z, z, z, rows, conv_w8, conv_w8, conv_w8,
      conv_buf8, conv_buf8, conv_buf8, s0, norm_w.reshape(1, hd))


def _attn_prompt_kernel(q_ref, k_ref, v_ref, gate_ref, bias_ref, qw_ref, kw_ref,
                        o_ref, kc_ref, kbuf, vbuf, qs, bias_s, *, tq, hg):
    i = pl.program_id(2)
    hd = HEAD_DIM
    npairs = tq // PAIR_ROWS
    problems = [(h, p) for h in range(hg) for p in range(npairs)]

    for h in range(hg):
        cols = slice(h * hd, (h + 1) * hd)

        @pl.when(i == 0)
        def _():
            kbuf[h, 0:PREV_ROWS, :] = jnp.zeros((PREV_ROWS, hd), BF16)
            vbuf[h, 0:PREV_ROWS, :] = jnp.zeros((PREV_ROWS, hd), BF16)

        @pl.when(i > 0)
        def _():
            kbuf[h, 0:PREV_ROWS, :] = kbuf[h, tq:tq + PREV_ROWS, :]
            vbuf[h, 0:PREV_ROWS, :] = vbuf[h, tq:tq + PREV_ROWS, :]

        kn = _rms(k_ref[0, :, cols].astype(F32), kw_ref[...])
        kc_ref[0, :, cols] = kn
        kbuf[h, PREV_ROWS:PREV_ROWS + tq, :] = kn.astype(BF16)
        vbuf[h, PREV_ROWS:PREV_ROWS + tq, :] = v_ref[0, :, cols]
        qs[h] = (_rms(q_ref[0, :, cols].astype(F32), qw_ref[...]) * (hd ** -0.5)).astype(BF16)

    @pl.when(i == 0)
    def _():
        col = lax.broadcasted_iota(jnp.int32, (PAIR_ROWS, PAIR_COLS), 1)
        for h, p in problems:
            bias_s[h, p] = jnp.where(col < PREV_ROWS - p * PAIR_ROWS, MASKED, bias_ref[h])

    @pl.when(i == 1)
    def _():
        for h, p in problems:
            bias_s[h, p] = bias_ref[h]

    def window(p):
        return slice(p * PAIR_ROWS, p * PAIR_ROWS + PAIR_COLS)

    def rows(p):
        return slice(p * PAIR_ROWS, (p + 1) * PAIR_ROWS)

    s = [_dot_nt(qs[h, rows(p), :], kbuf[h, window(p), :]) + bias_s[h, p] for h, p in problems]
    e = [jnp.exp(x - jnp.max(x, axis=-1, keepdims=True)) for x in s]
    denom = [jnp.sum(x, axis=-1, keepdims=True) for x in e]
    for n, (h, p) in enumerate(problems):
        cols = slice(h * hd, (h + 1) * hd)
        o = _dot(e[n].astype(BF16), vbuf[h, window(p), :]) / denom[n]
        o_ref[0, rows(p), cols] = (o * _silu(gate_ref[0, rows(p), cols].astype(F32))).astype(BF16)


def _attn_prompt_call(z, bias_tab, q_norm_w, k_norm_w, heads):
    batch, seq, _ = z.shape
    tq = PREV_ROWS
    hg = min(heads, ATT_HEADS_PER_STEP)
    assert seq % tq == 0 and tq % PAIR_ROWS == 0 and heads % hg == 0
    hd = HEAD_DIM
    wd = hg * hd
    ng = heads // hg

    def zcol(seg):
        return pl.BlockSpec((1, tq, wd), lambda b, g, i: (b, i, seg * ng + g))

    kern = functools.partial(_attn_prompt_kernel, tq=tq, hg=hg)
    return pl.pallas_call(
        kern,
        grid=(batch, ng, seq // tq),
        in_specs=[zcol(4), zcol(5), zcol(6), zcol(7),
                  pl.BlockSpec((hg, PAIR_ROWS, PAIR_COLS), lambda b, g, i: (g, 0, 0)),
                  pl.BlockSpec((1, hd), lambda b, g, i: (0, 0)),
                  pl.BlockSpec((1, hd), lambda b, g, i: (0, 0))],
        out_specs=[pl.BlockSpec((1, tq, wd), lambda b, g, i: (b, i, g)),
                   pl.BlockSpec((1, tq, wd), lambda b, g, i: (b, 0, g))],
        out_shape=[jax.ShapeDtypeStruct((batch, seq, heads * hd), BF16),
                   jax.ShapeDtypeStruct((batch, tq, heads * hd), F32)],
        scratch_shapes=[pltpu.VMEM((hg, PREV_ROWS + tq, hd), BF16),
                        pltpu.VMEM((hg, PREV_ROWS + tq, hd), BF16),
                        pltpu.VMEM((hg, tq, hd), BF16),
                        pltpu.VMEM((hg, tq // PAIR_ROWS, PAIR_ROWS, PAIR_COLS), F32)],
        compiler_params=_params("parallel", "parallel", "arbitrary"),
        name="band_attention_prompt",
    )(z, z, z, z, bias_tab, q_norm_w.reshape(1, hd), k_norm_w.reshape(1, hd))


def _attn_sample_kernel(q_ref, k_ref, v_ref, gate_ref, ck_ref, cv_ref, bias_ref, qw_ref, kw_ref,
                        o_ref, kn_ref, *, seq, n_cached, heads):
    hd = HEAD_DIM
    for h in range(heads):
        cols = slice(h * hd, (h + 1) * hd)
        kn = _rms(k_ref[0, :, cols].astype(F32), kw_ref[...])
        kn_ref[0, :, cols] = kn
        q16 = (_rms(q_ref[0, :, cols].astype(F32), qw_ref[...]) * (hd ** -0.5)).astype(BF16)
        bias = bias_ref[h]
        s_old = _dot_nt(q16, ck_ref[0, :, cols].astype(BF16)) + bias[:seq, :n_cached]
        s_new = _dot_nt(q16, kn.astype(BF16)) + bias[:seq, n_cached:n_cached + seq]
        m = jnp.maximum(jnp.max(s_old, axis=-1, keepdims=True), jnp.max(s_new, axis=-1, keepdims=True))
        p_old = jnp.exp(s_old - m)
        p_new = jnp.exp(s_new - m)
        denom = jnp.sum(p_old, axis=-1, keepdims=True) + jnp.sum(p_new, axis=-1, keepdims=True)
        o = (_dot(p_old.astype(BF16), cv_ref[0, :, cols].astype(BF16))
             + _dot(p_new.astype(BF16), v_ref[0, :, cols])) / denom
        o_ref[0, :, cols] = (o * _silu(gate_ref[0, :, cols].astype(F32))).astype(BF16)


def _attn_sample_call(z, cache_k, cache_v, bias_tab, q_norm_w, k_norm_w, heads):
    batch, seq, _ = z.shape
    n_cached = cache_k.shape[1]
    hd = HEAD_DIM
    wd = heads * hd
    q_pos = PAST_LEN + np.arange(seq)
    k_pos = PAST_LEN - n_cached + np.arange(n_cached + seq)
    qc, kc = q_pos // CHUNK, k_pos // CHUNK
    valid = (k_pos[None] >= 0) & (kc[None] <= qc[:, None]) & (kc[None] >= qc[:, None] - BAND_PREV)
    assert valid.all() and n_cached == PREV_ROWS and seq <= CHUNK and n_cached + seq <= BAND

    def zcol(seg):
        return pl.BlockSpec((1, seq, wd), lambda b: (b, 0, seg))

    cache_spec = pl.BlockSpec((1, n_cached, wd), lambda b: (b, 0, 0))
    kern = functools.partial(_attn_sample_kernel, seq=seq, n_cached=n_cached, heads=heads)
    return pl.pallas_call(
        kern,
        grid=(batch,),
        in_specs=[zcol(4), zcol(5), zcol(6), zcol(7),
                  cache_spec, cache_spec,
                  pl.BlockSpec((heads, PAIR_ROWS, PAIR_COLS), lambda b: (0, 0, 0)),
                  pl.BlockSpec((1, hd), lambda b: (0, 0)),
                  pl.BlockSpec((1, hd), lambda b: (0, 0))],
        out_specs=[pl.BlockSpec((1, seq, wd), lambda b: (b, 0, 0)),
                   pl.BlockSpec((1, seq, wd), lambda b: (b, 0, 0))],
        out_shape=[jax.ShapeDtypeStruct((batch, seq, wd), BF16),
                   jax.ShapeDtypeStruct((batch, seq, wd), F32)],
        compiler_params=_params("parallel"),
        name="band_attention_sample",
    )(z, z, z, z, cache_k.reshape(batch, n_cached, wd),
      cache_v.reshape(batch, n_cached, wd), bias_tab,
      q_norm_w.reshape(1, hd), k_norm_w.reshape(1, hd))


def _out_kernel(oa_ref, ob_ref, ma_ref, mb_ref, x_ref, g_ref, wa_ref, wb_ref, wo_ref, y_ref):
    bb, tt, d = x_ref.shape
    ya = _dot(oa_ref[...].reshape(bb * tt, -1), wa_ref[...])
    yb = _dot(ob_ref[...].reshape(bb * tt, -1), wb_ref[...])
    merged = (_sigmoid(ma_ref[...].reshape(bb * tt, d).astype(F32)) * ya
              + _sigmoid(mb_ref[...].reshape(bb * tt, d).astype(F32)) * yb)
    delta = _dot(merged.astype(BF16), wo_ref[...]).reshape(bb, tt, d)
    y_ref[...] = x_ref[...] + g_ref[...] * delta


def _out_call(o_a, o_b, z, x, gate, w_proj_a, w_proj_b, w_out, heads, layer):
    batch, seq, d = x.shape
    bb, tt = _row_tiling(batch, seq, 512)
    wa = o_a.shape[-1]
    wb = o_b.shape[-1]
    m_off = (8 * heads * HEAD_DIM) // d
    assert m_off * d == 8 * heads * HEAD_DIM

    def rows(width, off=0):
        return pl.BlockSpec((bb, tt, width), lambda b, i: (b, i, off))

    def whole(shape):
        return pl.BlockSpec((None,) + shape, lambda b, i: (layer, 0, 0), pipeline_mode=pl.Buffered(1))

    return pl.pallas_call(
        _out_kernel,
        grid=(batch // bb, seq // tt),
        in_specs=[rows(wa), rows(wb), rows(d, m_off), rows(d, m_off + 1), rows(d),
                  pl.BlockSpec((bb, 1, d), lambda b, i: (b, 0, 0)),
                  whole((wa, d)), whole((wb, d)), whole((d, d))],
        out_specs=rows(d),
        out_shape=jax.ShapeDtypeStruct((batch, seq, d), F32),
        compiler_params=_params("parallel", "parallel"),
        name="out_proj",
    )(o_a, o_b, z, z, x, gate, w_proj_a, w_proj_b, w_out)


def _layer(x, mod, conv_buf, s0, cache_k, cache_v, p, bias_tab, heads, prompt, layer):
    (norm_w, w_main, w_small, conv_w8, a_log, dt_bias, gdn_norm_w,
     q_norm_w, k_norm_w, w_proj_a, w_proj_b, w_out) = p
    batch, seq, d = x.shape
    hd = HEAD_DIM
    shift, scale, gate = [m.reshape(batch, 1, d) for m in jnp.split(mod, 3, axis=-1)]

    z, z_small = _inproj_call(x, scale, shift, norm_w, w_main, w_small, layer)

    logits = jnp.transpose(z_small[..., :2 * heads].reshape(batch, seq, 2, heads), (0, 3, 2, 1))
    rows = jnp.pad(logits, ((0, 0), (0, 0), (0, SUBLANES - 2), (0, 0)))
    conv_buf8 = jnp.pad(conv_buf, ((0, 0), (CONV_PAD - (CONV_WIDTH - 1), 0), (0, 0)))
    o_a, s_new = _gdn_call(z, rows, conv_w8, conv_buf8, s0, a_log, dt_bias, gdn_norm_w, heads)

    conv_dim = 3 * heads * hd
    new_buf = z[:, seq - (CONV_WIDTH - 1):, :conv_dim].astype(F32)
    if prompt:
        o_b, k_new = _attn_prompt_call(z, bias_tab, q_norm_w, k_norm_w, heads)
        keep = k_new.shape[1]
        v_new = z[:, seq - keep:, 6 * heads * hd:7 * heads * hd]
    else:
        o_b, k_new = _attn_sample_call(z, cache_k, cache_v, bias_tab, q_norm_w, k_norm_w, heads)
        keep = seq
        v_new = z[:, :, 6 * heads * hd:7 * heads * hd]
    k_new = k_new.reshape(batch, keep, heads, hd)
    v_new = v_new.reshape(batch, keep, heads, hd).astype(F32)

    y = _out_call(o_a, o_b, z, x, gate, w_proj_a, w_proj_b, w_out, heads, layer)
    return y, new_buf, s_new, k_new, v_new


def kernel(x_prompt, x_sample, c_prompt, c_sample, state_conv, state_delta, cache_k, cache_v, norm_w, ada_w, ada_b, w_in, conv_w, gdn_a_log, gdn_dt_bias, gdn_norm_w, q_norm_w, k_norm_w, rel_bias, w_proj_a, w_proj_b, w_out):
    bp, _, d = x_prompt.shape
    depth = w_in.shape[0]
    heads = gdn_a_log.shape[1]
    hd = HEAD_DIM
    small0 = 4 * heads * hd
    assert w_in.shape[2] == 8 * heads * hd + 2 * heads + 2 * d
    assert conv_w.shape[1:] == (CONV_WIDTH, 3 * heads * hd) and rel_bias.shape[1:] == (heads, N_REL)

    mod = _mod_call(jnp.concatenate([c_prompt, c_sample], axis=0), ada_w, ada_b)
    bias_tab = _bias_call(rel_bias)

    w16 = w_in.astype(BF16)
    w_main = jnp.concatenate([w16[:, :, :small0], w16[:, :, small0 + 2 * heads:]], axis=2)
    w_small = jnp.pad(w16[:, :, small0:small0 + 2 * heads], ((0, 0), (0, 0), (0, LANES - 2 * heads)))
    w_proj_a, w_proj_b, w_out = (w.astype(BF16) for w in (w_proj_a, w_proj_b, w_out))

    yp, ys = x_prompt, x_sample
    outs_p, outs_s = [], []
    zero_buf = jnp.zeros((bp, CONV_WIDTH - 1, 3 * heads * hd), x_prompt.dtype)
    zero_state = jnp.zeros((bp, heads, hd, hd), state_delta.dtype)
    for l in range(depth):
        conv_w8 = jnp.pad(conv_w[l], ((0, SUBLANES - CONV_WIDTH), (0, 0)))
        p = (norm_w[l], w_main, w_small, conv_w8, gdn_a_log[l], gdn_dt_bias[l], gdn_norm_w[l],
             q_norm_w[l], k_norm_w[l], w_proj_a, w_proj_b, w_out)
        yp, *rest = _layer(yp, mod[l, :bp], zero_buf, zero_state, None, None, p, bias_tab[l], heads, True, l)
        outs_p.append(rest)
        ys, *rest = _layer(ys, mod[l, bp:], state_conv[l], state_delta[l], cache_k[l], cache_v[l],
                           p, bias_tab[l], heads, False, l)
        outs_s.append(rest)

    stack = lambda outs, i: jnp.stack([o[i] for o in outs])
    return (yp, ys,
            stack(outs_p, 0), stack(outs_p, 1), stack(outs_p, 2), stack(outs_p, 3),
            stack(outs_s, 0), stack(outs_s, 1), stack(outs_s, 2), stack(outs_s, 3))
```

```python
import functools
import math

import numpy as np
import jax
import jax.numpy as jnp
from jax import lax
from jax.experimental import pallas as pl
from jax.experimental.pallas import tpu as pltpu

F32 = jnp.float32
BF16 = jnp.bfloat16

EPS = 1e-6
CHUNK = 64
BAND_PREV = 8
REL_CLIP = 128
N_REL = 2 * REL_CLIP + 1
PAST_LEN = 2048
CONV_WIDTH = 4
HEAD_DIM = 128
LANES = 128
SUBLANES = 8
VMEM_LIMIT_BYTES = 56 * 1024 * 1024
GDN_BLOCK = 64
GDN_SUPER = 256
GDN_HEADS_PER_STEP = 8
CONV_PAD = 2 * SUBLANES
PREV_ROWS = BAND_PREV * CHUNK
BAND = PREV_ROWS + CHUNK
PAIR_ROWS = 2 * CHUNK
PAIR_COLS = PREV_ROWS + PAIR_ROWS
MASKED = -1e30
ATT_HEADS_PER_STEP = 8


def _dot(a, b):
    return jnp.dot(a, b, preferred_element_type=F32)


def _dot_nt(a, b):
    return lax.dot_general(a, b, (((1,), (1,)), ((), ())), preferred_element_type=F32)


def _dot_tn(a, b):
    return lax.dot_general(a, b, (((0,), (0,)), ((), ())), preferred_element_type=F32)


def _sigmoid(x):
    return 0.5 + 0.5 * jnp.tanh(0.5 * x)


def _silu(x):
    half = 0.5 * x
    return half + half * jnp.tanh(half)


def _rms(x, w):
    return x * lax.rsqrt(jnp.mean(x * x, axis=-1, keepdims=True) + EPS) * w


def _params(*sem):
    return pltpu.CompilerParams(dimension_semantics=sem, vmem_limit_bytes=VMEM_LIMIT_BYTES)


def _largest_divisor(n, candidates):
    for c in candidates:
        if n % c == 0:
            return c
    raise ValueError(f"no tile in {candidates} divides {n}")


def _row_tiling(batch, seq, rows):
    if seq >= rows:
        assert seq % rows == 0
        return 1, rows
    bb = min(batch, rows // seq)
    assert batch % bb == 0
    return bb, seq


def _mod_kernel(c_ref, w0_ref, w1_ref, w2_ref, b_ref, o_ref):
    @pl.when(pl.program_id(1) == 0)
    def _():
        o_ref[0] = jnp.broadcast_to(b_ref[0], o_ref.shape[1:])

    act = _silu(c_ref[...]).astype(BF16)
    n3 = w0_ref.shape[-1]
    for t, w_ref in enumerate((w0_ref, w1_ref, w2_ref)):
        o_ref[0, :, t * n3:(t + 1) * n3] += _dot(act, w_ref[0].astype(BF16))


def _mod_call(c, ada_w, ada_b):
    depth, d, n = ada_w.shape
    rows = c.shape[0]
    tk = _largest_divisor(d, (256, 128))
    n3 = n // 3
    assert n3 * 3 == n and n3 % LANES == 0

    def third(t):
        return pl.BlockSpec((1, tk, n3), lambda l, k: (l, k, t))

    return pl.pallas_call(
        _mod_kernel,
        grid=(depth, d // tk),
        in_specs=[pl.BlockSpec((rows, tk), lambda l, k: (0, k)),
                  third(0), third(1), third(2),
                  pl.BlockSpec((1, 1, n), lambda l, k: (l, 0, 0))],
        out_specs=pl.BlockSpec((1, rows, n), lambda l, k: (l, 0, 0)),
        out_shape=jax.ShapeDtypeStruct((depth, rows, n), F32),
        compiler_params=_params("parallel", "arbitrary"),
        name="adaln_mod",
    )(c, ada_w, ada_w, ada_w, ada_b.reshape(depth, 1, n))


def _bias_kernel(rb_ref, o_ref):
    rb = rb_ref[0]
    heads, nrel = rb.shape
    rb_hi = rb.astype(BF16)
    rest = rb - rb_hi.astype(F32)
    rb_mid = rest.astype(BF16)
    rb_lo = (rest - rb_mid.astype(F32)).astype(BF16)
    r = lax.broadcasted_iota(jnp.int32, (nrel, PAIR_COLS), 0)
    j = lax.broadcasted_iota(jnp.int32, (nrel, PAIR_COLS), 1)
    onehot = (jnp.clip(PREV_ROWS - j, -REL_CLIP, REL_CLIP) + REL_CLIP == r).astype(BF16)
    row0 = (_dot(rb_hi, onehot) + _dot(rb_mid, onehot)) + _dot(rb_lo, onehot)
    lane = lax.broadcasted_iota(jnp.int32, (heads, PAIR_COLS), 1)
    key_chunk = lane // CHUNK
    assert PREV_ROWS >= REL_CLIP
    far = rb[:, N_REL - 1:N_REL]

    def body(i, row):
        q_chunk = i // CHUNK
        in_band = (key_chunk >= q_chunk) & (key_chunk <= q_chunk + BAND_PREV)
        o_ref[0, i] = jnp.where(in_band, row, MASKED)
        return jnp.where(lane == 0, far, pltpu.roll(row, 1, axis=1))

    lax.fori_loop(0, PAIR_ROWS, body, row0)


def _bias_call(rel_bias):
    depth, heads, nrel = rel_bias.shape
    nrel_pad = -(-nrel // SUBLANES) * SUBLANES
    rb = jnp.pad(rel_bias, ((0, 0), (0, 0), (0, nrel_pad - nrel)))
    tab = pl.pallas_call(
        _bias_kernel,
        grid=(depth,),
        in_specs=[pl.BlockSpec((1, heads, nrel_pad), lambda l: (l, 0, 0))],
        out_specs=pl.BlockSpec((1, PAIR_ROWS, heads, PAIR_COLS), lambda l: (l, 0, 0, 0)),
        out_shape=jax.ShapeDtypeStruct((depth, PAIR_ROWS, heads, PAIR_COLS), F32),
        compiler_params=_params("parallel"),
        name="rel_bias_table",
    )(rb)
    return jnp.transpose(tab, (0, 2, 1, 3))


def _inproj_kernel(x_ref, sc_ref, sh_ref, nw_ref, w_ref, ws_ref, z_ref, zs_ref, h_ref, *, bb, tt, rc):
    @pl.when(pl.program_id(2) == 0)
    def _():
        for b in range(bb):
            gain = nw_ref[...] * (1.0 + sc_ref[b])
            for r0 in range(0, tt, rc):
                x = x_ref[b, r0:r0 + rc, :]
                inv = lax.rsqrt(jnp.mean(x * x, axis=-1, keepdims=True) + EPS)
                h_ref[b * tt + r0:b * tt + r0 + rc, :] = ((x * inv) * gain + sh_ref[b]).astype(BF16)
        zs_ref[...] = _dot(h_ref[...], ws_ref[...]).reshape(zs_ref.shape)

    z_ref[...] = _dot(h_ref[...], w_ref[...]).astype(z_ref.dtype).reshape(z_ref.shape)


def _inproj_call(x, scale, shift, norm_w, w_main, w_small, layer):
    batch, seq, d = x.shape
    n = w_main.shape[2]
    ns = w_small.shape[2]
    bb, tt = _row_tiling(batch, seq, 1024)
    tn = _largest_divisor(n, (2048, 1024, 512, 256, 128))
    kern = functools.partial(_inproj_kernel, bb=bb, tt=tt, rc=min(tt, 256))
    return pl.pallas_call(
        kern,
        grid=(batch // bb, seq // tt, n // tn),
        in_specs=[pl.BlockSpec((bb, tt, d), lambda b, i, j: (b, i, 0)),
                  pl.BlockSpec((bb, 1, d), lambda b, i, j: (b, 0, 0)),
                  pl.BlockSpec((bb, 1, d), lambda b, i, j: (b, 0, 0)),
                  pl.BlockSpec((1, d), lambda b, i, j: (0, 0)),
                  pl.BlockSpec((None, d, tn), lambda b, i, j: (layer, 0, j)),
                  pl.BlockSpec((None, d, ns), lambda b, i, j: (layer, 0, 0))],
        out_specs=[pl.BlockSpec((bb, tt, tn), lambda b, i, j: (b, i, j)),
                   pl.BlockSpec((bb, tt, ns), lambda b, i, j: (b, i, 0))],
        out_shape=[jax.ShapeDtypeStruct((batch, seq, n), BF16),
                   jax.ShapeDtypeStruct((batch, seq, ns), F32)],
        scratch_shapes=[pltpu.VMEM((bb * tt, d), BF16)],
        compiler_params=_params("parallel", "parallel", "arbitrary"),
        name="in_proj",
    )(x, scale, shift, norm_w.reshape(1, d), w_main, w_small)


def _gdn_kernel(alog_ref, dt_ref, q_ref, k_ref, v_ref, gate_ref, r_ref,
                cwq_ref, cwk_ref, cwv_ref, csq_ref, csk_ref, csv_ref, s0_ref, nw_ref,
                o_ref, sfin_ref, xs_ref, s_ref, *, tb, blk, hg):
    head0 = pl.program_id(1) * hg
    t = pl.program_id(2)
    pad = CONV_PAD
    hd = HEAD_DIM
    heads = range(hg)
    assert CONV_WIDTH == 4

    @pl.when(t == 0)
    def _():
        xs_ref[0, 0:pad, :] = csq_ref[0]
        xs_ref[1, 0:pad, :] = csk_ref[0]
        xs_ref[2, 0:pad, :] = csv_ref[0]
        s_ref[...] = s0_ref[0]

    xs_ref[0, pad:pad + tb, :] = q_ref[0].astype(F32)
    xs_ref[1, pad:pad + tb, :] = k_ref[0].astype(F32)
    xs_ref[2, pad:pad + tb, :] = v_ref[0].astype(F32)

    def conv_act(i, cw_ref, h):
        cols = slice(h * hd, (h + 1) * hd)
        w = cw_ref[:, cols]
        groups = (pad + tb) // SUBLANES
        sub = lax.broadcasted_iota(jnp.int32, (1, SUBLANES, hd), 1)

        def shifted(a, n):
            r = pltpu.roll(a, n, axis=1)
            return jnp.where(sub >= n, r[1:], r[:-1])

        x = xs_ref[i, :, cols].reshape(groups, SUBLANES, hd)
        x1 = shifted(x, 1)
        late = x[2:] * w[3:4] + x1[1:] * w[2:3]
        early = x[1:] * w[1:2] + x1 * w[0:1]
        return _silu(late + shifted(early, 2)).reshape(tb, hd)

    lt = min(LANES, tb)
    nblk = tb // blk

    def slab_of(mat, b):
        c0 = (b * blk) // lt * lt
        return mat[b * blk:(b + 1) * blk, c0:c0 + lt]

    def full_of(slabs):
        rows = []
        for b, sl in enumerate(slabs):
            tile = (b * blk) // lt
            parts = [sl if c == tile else jnp.zeros((blk, lt), sl.dtype) for c in range(tb // lt)]
            rows.append(parts[0] if len(parts) == 1 else jnp.concatenate(parts, axis=1))
        return rows[0] if len(rows) == 1 else jnp.concatenate(rows, axis=0)

    incl, strict, eye = [], [], []
    for b in range(nblk):
        ri = lax.broadcasted_iota(jnp.int32, (blk, lt), 0) + b * blk
        ci = lax.broadcasted_iota(jnp.int32, (blk, lt), 1) + (b * blk) // lt * lt
        same = (ci >= b * blk) & (ci < (b + 1) * blk)
        incl.append(same & (ri >= ci))
        strict.append(same & (ri > ci))
        eye.append((ri == ci).astype(F32))

    ri = lax.broadcasted_iota(jnp.int32, (tb, tb), 0)
    ci = lax.broadcasted_iota(jnp.int32, (tb, tb), 1)
    shift = int(math.log2(blk))
    same = (ri >> shift) == (ci >> shift)
    sum_mats = jnp.concatenate([(same & (ri <= ci)).astype(BF16), same.astype(BF16)], axis=1)
    row_id = lax.broadcasted_iota(jnp.int32, (SUBLANES, tb), 0)

    def split16(x):
        hi = x.astype(BF16)
        return hi, (x - hi.astype(F32)).astype(BF16)

    r = r_ref[0].reshape(hg * SUBLANES, tb)
    dt_rows = jnp.concatenate([jnp.full((SUBLANES, tb), dt_ref[head0 + h], F32) for h in heads], axis=0)
    alog_rows = jnp.concatenate([jnp.full((SUBLANES, tb), alog_ref[head0 + h], F32) for h in heads], axis=0)
    beta_rows = _sigmoid(r)
    xx = r + dt_rows
    softplus = jnp.maximum(xx, 0.0) + jnp.log1p(jnp.exp(-jnp.abs(xx)))
    g_rows = -jnp.exp(alog_rows) * softplus
    g_hi = g_rows.astype(BF16)
    g_rest = g_rows - g_hi.astype(F32)
    g_mid = g_rest.astype(BF16)
    g_lo = (g_rest - g_mid.astype(F32)).astype(BF16)
    sums = (_dot(g_hi, sum_mats) + _dot(g_mid, sum_mats)) + _dot(g_lo, sum_mats)

    q, k, v, neg_m, a_intra, beta_c, g_c, g_last_c = [], [], [], [], [], [], [], []
    for h in heads:
        aq = conv_act(0, cwq_ref, h)
        ak = conv_act(1, cwk_ref, h)
        v.append(conv_act(2, cwv_ref, h))
        q.append(aq * (lax.rsqrt(jnp.sum(aq * aq, axis=-1, keepdims=True) + EPS) * (hd ** -0.5)))
        k.append(ak * lax.rsqrt(jnp.sum(ak * ak, axis=-1, keepdims=True) + EPS))

        hs = slice(h * SUBLANES, (h + 1) * SUBLANES)
        g_cum_rows = sums[hs, :tb]
        cols = jnp.transpose(jnp.where(row_id == 0, beta_rows[hs], g_cum_rows))
        beta_c.append(cols[:, 0:1])
        g_c.append(cols[:, 1:2])
        g_last_c.append(jnp.transpose(sums[hs, tb:])[:, 1:2])
        g_r = g_cum_rows[1:2, :]

        k16 = k[h].astype(BF16)
        kk = _dot_nt(k16, k16)
        qk = _dot_nt(q[h].astype(BF16), k16)
        neg_m_h, a_h = [], []
        for b in range(nblk):
            rs = slice(b * blk, (b + 1) * blk)
            c0 = (b * blk) // lt * lt
            diff = g_c[h][rs] - g_r[:, c0:c0 + lt]
            decay = jnp.where(incl[b], jnp.exp(jnp.where(incl[b], diff, 0.0)), 0.0)
            neg_m_h.append(jnp.where(strict[b], -(beta_c[h][rs] * slab_of(kk, b)) * decay, 0.0))
            a_h.append((slab_of(qk, b) * decay).astype(BF16))
        neg_m.append(neg_m_h)
        a_intra.append(full_of(a_h))

    for i in range(3):
        xs_ref[i, 0:pad, :] = xs_ref[i, tb:tb + pad, :]

    n_hi, n_lo = [], []
    for h in heads:
        pieces = [split16(m) for m in neg_m[h]]
        n_hi.append(full_of([p[0] for p in pieces]))
        n_lo.append(full_of([p[1] for p in pieces]))
    x = [[eye[b] + neg_m[h][b] for b in range(nblk)] for h in heads]
    steps = shift - 1
    for step in range(steps):
        if step < steps - 1:
            x16 = [full_of([xb.astype(BF16) for xb in x[h]]) for h in heads]
            nx = [_dot(n_hi[h], x16[h]) for h in heads]
        else:
            pieces = [[split16(xb) for xb in x[h]] for h in heads]
            x16 = [full_of([p[0] for p in pieces[h]]) for h in heads]
            x_lo = [full_of([p[1] for p in pieces[h]]) for h in heads]
            nx = [_dot(n_hi[h], x16[h]) + (_dot(n_hi[h], x_lo[h]) + _dot(n_lo[h], x16[h])) for h in heads]
        resid = [full_of([((eye[b] - x[h][b]) + slab_of(nx[h], b)).astype(BF16) for b in range(nblk)])
                 for h in heads]
        upd = [_dot(x16[h], resid[h]) for h in heads]
        x = [[x[h][b] + slab_of(upd[h], b) for b in range(nblk)] for h in heads]
    x = [full_of([xb.astype(BF16) for xb in x[h]]) for h in heads]

    u, w, qg, kg, e_last = [], [], [], [], []
    for h in heads:
        e_g = jnp.exp(g_c[h])
        rhs = jnp.concatenate([v[h] * beta_c[h], k[h] * (beta_c[h] * e_g)], axis=1).astype(BF16)
        uw = _dot(x[h], rhs)
        u.append(uw[:, :hd])
        w.append(uw[:, hd:])
        qg.append(q[h] * e_g)
        kg.append((k[h] * jnp.exp(g_last_c[h] - g_c[h])).astype(BF16))
        e_last.append(jnp.exp(g_last_c[h]))

    s = [s_ref[h] for h in heads]
    v_new = [[] for _ in heads]
    o_inter = [[] for _ in heads]
    for b in range(tb // blk):
        rs = slice(b * blk, (b + 1) * blk)
        for h in heads:
            ws_qs = _dot(jnp.concatenate([w[h][rs], qg[h][rs]], axis=0).astype(BF16), s[h].astype(BF16))
            vn = u[h][rs] - ws_qs[:blk]
            v_new[h].append(vn)
            o_inter[h].append(ws_qs[blk:])
            s[h] = s[h] * e_last[h][b * blk:b * blk + 1, :] + _dot_tn(kg[h][rs], vn.astype(BF16))

    for h in heads:
        s_ref[h] = s[h]
        cols = slice(h * hd, (h + 1) * hd)
        vn = jnp.concatenate(v_new[h], axis=0).astype(BF16)
        o = jnp.concatenate(o_inter[h], axis=0) + _dot(a_intra[h], vn)
        o_ref[0, :, cols] = (_rms(o, nw_ref[...]) * _silu(gate_ref[0, :, cols].astype(F32))).astype(BF16)

    @pl.when(t == pl.num_programs(2) - 1)
    def _():
        sfin_ref[0] = s_ref[...]


def _gdn_call(z, rows, conv_w8, conv_buf8, s0, a_log, dt_bias, norm_w, heads):
    batch, seq, _ = z.shape
    tb = min(seq, GDN_SUPER)
    blk = min(tb, GDN_BLOCK)
    hg = min(heads, GDN_HEADS_PER_STEP)
    assert seq % tb == 0 and tb % blk == 0 and heads % hg == 0
    hd = HEAD_DIM
    wd = hg * hd
    ng = heads // hg
    smem = pl.BlockSpec(memory_space=pltpu.SMEM)

    def zcol(seg):
        return pl.BlockSpec((1, tb, wd), lambda b, g, t: (b, t, seg * ng + g))

    def cw(seg):
        return pl.BlockSpec((SUBLANES, wd), lambda b, g, t: (0, seg * ng + g))

    def cs(seg):
        return pl.BlockSpec((1, CONV_PAD, wd), lambda b, g, t: (b, 0, seg * ng + g))

    state_spec = pl.BlockSpec((1, hg, hd, hd), lambda b, g, t: (b, g, 0, 0))
    kern = functools.partial(_gdn_kernel, tb=tb, blk=blk, hg=hg)
    return pl.pallas_call(
        kern,
        grid=(batch, ng, seq // tb),
        in_specs=[smem, smem,
                  zcol(0), zcol(1), zcol(2), zcol(3),
                  pl.BlockSpec((1, hg, SUBLANES, tb), lambda b, g, t: (b, g, 0, t)),
                  cw(0), cw(1), cw(2),
                  cs(0), cs(1), cs(2),
                  state_spec,
                  pl.BlockSpec((1, hd), lambda b, g, t: (0, 0))],
        out_specs=[pl.BlockSpec((1, tb, wd), lambda b, g, t: (b, t, g)),
                   state_spec],
        out_shape=[jax.ShapeDtypeStruct((batch, seq, heads * hd), BF16),
                   jax.ShapeDtypeStruct((batch, heads, hd, hd), F32)],
        scratch_shapes=[pltpu.VMEM((3, tb + CONV_PAD, wd), F32),
                        pltpu.VMEM((hg, hd, hd), F32)],
        compiler_params=_params("parallel", "parallel", "arbitrary"),
        name="gated_deltanet",
    )(a_log, dt_bias, z, z, z, z, rows, conv_w8, conv_w8, conv_w8,
      conv_buf8, conv_buf8, conv_buf8, s0, norm_w.reshape(1, hd))


def _attn_prompt_kernel(q_ref, k_ref, v_ref, gate_ref, bias_ref, qw_ref, kw_ref,
                        o_ref, kc_ref, kbuf, vbuf, qs, bias_s, *, tq, hg):
    i = pl.program_id(2)
    hd = HEAD_DIM
    npairs = tq // PAIR_ROWS
    problems = [(h, p) for h in range(hg) for p in range(npairs)]

    for h in range(hg):
        cols = slice(h * hd, (h + 1) * hd)

        @pl.when(i == 0)
        def _():
            kbuf[h, 0:PREV_ROWS, :] = jnp.zeros((PREV_ROWS, hd), BF16)
            vbuf[h, 0:PREV_ROWS, :] = jnp.zeros((PREV_ROWS, hd), BF16)

        @pl.when(i > 0)
        def _():
            kbuf[h, 0:PREV_ROWS, :] = kbuf[h, tq:tq + PREV_ROWS, :]
            vbuf[h, 0:PREV_ROWS, :] = vbuf[h, tq:tq + PREV_ROWS, :]

        kn = _rms(k_ref[0, :, cols].astype(F32), kw_ref[...])
        kc_ref[0, :, cols] = kn
        kbuf[h, PREV_ROWS:PREV_ROWS + tq, :] = kn.astype(BF16)
        vbuf[h, PREV_ROWS:PREV_ROWS + tq, :] = v_ref[0, :, cols]
        qs[h] = (_rms(q_ref[0, :, cols].astype(F32), qw_ref[...]) * (hd ** -0.5)).astype(BF16)

    @pl.when(i == 0)
    def _():
        col = lax.broadcasted_iota(jnp.int32, (PAIR_ROWS, PAIR_COLS), 1)
        for h, p in problems:
            bias_s[h, p] = jnp.where(col < PREV_ROWS - p * PAIR_ROWS, MASKED, bias_ref[h])

    @pl.when(i == 1)
    def _():
        for h, p in problems:
            bias_s[h, p] = bias_ref[h]

    def window(p):
        return slice(p * PAIR_ROWS, p * PAIR_ROWS + PAIR_COLS)

    def rows(p):
        return slice(p * PAIR_ROWS, (p + 1) * PAIR_ROWS)

    s = [_dot_nt(qs[h, rows(p), :], kbuf[h, window(p), :]) + bias_s[h, p] for h, p in problems]
    e = [jnp.exp(x - jnp.max(x, axis=-1, keepdims=True)) for x in s]
    denom = [jnp.sum(x, axis=-1, keepdims=True) for x in e]
    for n, (h, p) in enumerate(problems):
        cols = slice(h * hd, (h + 1) * hd)
        o = _dot(e[n].astype(BF16), vbuf[h, window(p), :]) / denom[n]
        o_ref[0, rows(p), cols] = (o * _silu(gate_ref[0, rows(p), cols].astype(F32))).astype(BF16)


def _attn_prompt_call(z, bias_tab, q_norm_w, k_norm_w, heads):
    batch, seq, _ = z.shape
    tq = PREV_ROWS
    hg = min(heads, ATT_HEADS_PER_STEP)
    assert seq % tq == 0 and tq % PAIR_ROWS == 0 and heads % hg == 0
    hd = HEAD_DIM
    wd = hg * hd
    ng = heads // hg

    def zcol(seg):
        return pl.BlockSpec((1, tq, wd), lambda b, g, i: (b, i, seg * ng + g))

    kern = functools.partial(_attn_prompt_kernel, tq=tq, hg=hg)
    return pl.pallas_call(
        kern,
        grid=(batch, ng, seq // tq),
        in_specs=[zcol(4), zcol(5), zcol(6), zcol(7),
                  pl.BlockSpec((hg, PAIR_ROWS, PAIR_COLS), lambda b, g, i: (g, 0, 0)),
                  pl.BlockSpec((1, hd), lambda b, g, i: (0, 0)),
                  pl.BlockSpec((1, hd), lambda b, g, i: (0, 0))],
        out_specs=[pl.BlockSpec((1, tq, wd), lambda b, g, i: (b, i, g)),
                   pl.BlockSpec((1, tq, wd), lambda b, g, i: (b, 0, g))],
        out_shape=[jax.ShapeDtypeStruct((batch, seq, heads * hd), BF16),
                   jax.ShapeDtypeStruct((batch, tq, heads * hd), F32)],
        scratch_shapes=[pltpu.VMEM((hg, PREV_ROWS + tq, hd), BF16),
                        pltpu.VMEM((hg, PREV_ROWS + tq, hd), BF16),
                        pltpu.VMEM((hg, tq, hd), BF16),
                        pltpu.VMEM((hg, tq // PAIR_ROWS, PAIR_ROWS, PAIR_COLS), F32)],
        compiler_params=_params("parallel", "parallel", "arbitrary"),
        name="band_attention_prompt",
    )(z, z, z, z, bias_tab, q_norm_w.reshape(1, hd), k_norm_w.reshape(1, hd))


def _attn_sample_kernel(q_ref, k_ref, v_ref, gate_ref, ck_ref, cv_ref, bias_ref, qw_ref, kw_ref,
                        o_ref, kn_ref, *, seq, n_cached, heads):
    hd = HEAD_DIM
    for h in range(heads):
        cols = slice(h * hd, (h + 1) * hd)
        kn = _rms(k_ref[0, :, cols].astype(F32), kw_ref[...])
        kn_ref[0, :, cols] = kn
        q16 = (_rms(q_ref[0, :, cols].astype(F32), qw_ref[...]) * (hd ** -0.5)).astype(BF16)
        bias = bias_ref[h]
        s_old = _dot_nt(q16, ck_ref[0, :, cols].astype(BF16)) + bias[:seq, :n_cached]
        s_new = _dot_nt(q16, kn.astype(BF16)) + bias[:seq, n_cached:n_cached + seq]
        m = jnp.maximum(jnp.max(s_old, axis=-1, keepdims=True), jnp.max(s_new, axis=-1, keepdims=True))
        p_old = jnp.exp(s_old - m)
        p_new = jnp.exp(s_new - m)
        denom = jnp.sum(p_old, axis=-1, keepdims=True) + jnp.sum(p_new, axis=-1, keepdims=True)
        o = (_dot(p_old.astype(BF16), cv_ref[0, :, cols].astype(BF16))
             + _dot(p_new.astype(BF16), v_ref[0, :, cols])) / denom
        o_ref[0, :, cols] = (o * _silu(gate_ref[0, :, cols].astype(F32))).astype(BF16)


def _attn_sample_call(z, cache_k, cache_v, bias_tab, q_norm_w, k_norm_w, heads):
    batch, seq, _ = z.shape
    n_cached = cache_k.shape[1]
    hd = HEAD_DIM
    wd = heads * hd
    q_pos = PAST_LEN + np.arange(seq)
    k_pos = PAST_LEN - n_cached + np.arange(n_cached + seq)
    qc, kc = q_pos // CHUNK, k_pos // CHUNK
    valid = (k_pos[None] >= 0) & (kc[None] <= qc[:, None]) & (kc[None] >= qc[:, None] - BAND_PREV)
    assert valid.all() and n_cached == PREV_ROWS and seq <= CHUNK and n_cached + seq <= BAND

    def zcol(seg):
        return pl.BlockSpec((1, seq, wd), lambda b: (b, 0, seg))

    cache_spec = pl.BlockSpec((1, n_cached, wd), lambda b: (b, 0, 0))
    kern = functools.partial(_attn_sample_kernel, seq=seq, n_cached=n_cached, heads=heads)
    return pl.pallas_call(
        kern,
        grid=(batch,),
        in_specs=[zcol(4), zcol(5), zcol(6), zcol(7),
                  cache_spec, cache_spec,
                  pl.BlockSpec((heads, PAIR_ROWS, PAIR_COLS), lambda b: (0, 0, 0)),
                  pl.BlockSpec((1, hd), lambda b: (0, 0)),
                  pl.BlockSpec((1, hd), lambda b: (0, 0))],
        out_specs=[pl.BlockSpec((1, seq, wd), lambda b: (b, 0, 0)),
                   pl.BlockSpec((1, seq, wd), lambda b: (b, 0, 0))],
        out_shape=[jax.ShapeDtypeStruct((batch, seq, wd), BF16),
                   jax.ShapeDtypeStruct((batch, seq, wd), F32)],
        compiler_params=_params("parallel"),
        name="band_attention_sample",
    )(z, z, z, z, cache_k.reshape(batch, n_cached, wd),
      cache_v.reshape(batch, n_cached, wd), bias_tab,
      q_norm_w.reshape(1, hd), k_norm_w.reshape(1, hd))


def _out_kernel(oa_ref, ob_ref, ma_ref, mb_ref, x_ref, g_ref, wa_ref, wb_ref, wo_ref, y_ref):
    bb, tt, d = x_ref.shape
    ya = _dot(oa_ref[...].reshape(bb * tt, -1), wa_ref[...])
    yb = _dot(ob_ref[...].reshape(bb * tt, -1), wb_ref[...])
    merged = (_sigmoid(ma_ref[...].reshape(bb * tt, d).astype(F32)) * ya
              + _sigmoid(mb_ref[...].reshape(bb * tt, d).astype(F32)) * yb)
    delta = _dot(merged.astype(BF16), wo_ref[...]).reshape(bb, tt, d)
    y_ref[...] = x_ref[...] + g_ref[...] * delta


def _out_call(o_a, o_b, z, x, gate, w_proj_a, w_proj_b, w_out, heads, layer):
    batch, seq, d = x.shape
    bb, tt = _row_tiling(batch, seq, 512)
    wa = o_a.shape[-1]
    wb = o_b.shape[-1]
    m_off = (8 * heads * HEAD_DIM) // d
    assert m_off * d == 8 * heads * HEAD_DIM

    def rows(width, off=0):
        return pl.BlockSpec((bb, tt, width), lambda b, i: (b, i, off))

    def whole(shape):
        return pl.BlockSpec((None,) + shape, lambda b, i: (layer, 0, 0), pipeline_mode=pl.Buffered(1))

    return pl.pallas_call(
        _out_kernel,
        grid=(batch // bb, seq // tt),
        in_specs=[rows(wa), rows(wb), rows(d, m_off), rows(d, m_off + 1), rows(d),
                  pl.BlockSpec((bb, 1, d), lambda b, i: (b, 0, 0)),
                  whole((wa, d)), whole((wb, d)), whole((d, d))],
        out_specs=rows(d),
        out_shape=jax.ShapeDtypeStruct((batch, seq, d), F32),
        compiler_params=_params("parallel", "parallel"),
        name="out_proj",
    )(o_a, o_b, z, z, x, gate, w_proj_a, w_proj_b, w_out)


def _layer(x, mod, conv_buf, s0, cache_k, cache_v, p, bias_tab, heads, prompt, layer):
    (norm_w, w_main, w_small, conv_w8, a_log, dt_bias, gdn_norm_w,
     q_norm_w, k_norm_w, w_proj_a, w_proj_b, w_out) = p
    batch, seq, d = x.shape
    hd = HEAD_DIM
    shift, scale, gate = [m.reshape(batch, 1, d) for m in jnp.split(mod, 3, axis=-1)]

    z, z_small = _inproj_call(x, scale, shift, norm_w, w_main, w_small, layer)

    logits = jnp.transpose(z_small[..., :2 * heads].reshape(batch, seq, 2, heads), (0, 3, 2, 1))
    rows = jnp.pad(logits, ((0, 0), (0, 0), (0, SUBLANES - 2), (0, 0)))
    conv_buf8 = jnp.pad(conv_buf, ((0, 0), (CONV_PAD - (CONV_WIDTH - 1), 0), (0, 0)))
    o_a, s_new = _gdn_call(z, rows, conv_w8, conv_buf8, s0, a_log, dt_bias, gdn_norm_w, heads)

    conv_dim = 3 * heads * hd
    new_buf = z[:, seq - (CONV_WIDTH - 1):, :conv_dim].astype(F32)
    if prompt:
        o_b, k_new = _attn_prompt_call(z, bias_tab, q_norm_w, k_norm_w, heads)
        keep = k_new.shape[1]
        v_new = z[:, seq - keep:, 6 * heads * hd:7 * heads * hd]
    else:
        o_b, k_new = _attn_sample_call(z, cache_k, cache_v, bias_tab, q_norm_w, k_norm_w, heads)
        keep = seq
        v_new = z[:, :, 6 * heads * hd:7 * heads * hd]
    k_new = k_new.reshape(batch, keep, heads, hd)
    v_new = v_new.reshape(batch, keep, heads, hd).astype(F32)

    y = _out_call(o_a, o_b, z, x, gate, w_proj_a, w_proj_b, w_out, heads, layer)
    return y, new_buf, s_new, k_new, v_new


def kernel(x_prompt, x_sample, c_prompt, c_sample, state_conv, state_delta, cache_k, cache_v, norm_w, ada_w, ada_b, w_in, conv_w, gdn_a_log, gdn_dt_bias, gdn_norm_w, q_norm_w, k_norm_w, rel_bias, w_proj_a, w_proj_b, w_out):
    bp, _, d = x_prompt.shape
    depth = w_in.shape[0]
    heads = gdn_a_log.shape[1]
    hd = HEAD_DIM
    small0 = 4 * heads * hd
    assert w_in.shape[2] == 8 * heads * hd + 2 * heads + 2 * d
    assert conv_w.shape[1:] == (CONV_WIDTH, 3 * heads * hd) and rel_bias.shape[1:] == (heads, N_REL)

    mod = _mod_call(jnp.concatenate([c_prompt, c_sample], axis=0), ada_w, ada_b)
    bias_tab = _bias_call(rel_bias)

    w16 = w_in.astype(BF16)
    w_main = jnp.concatenate([w16[:, :, :small0], w16[:, :, small0 + 2 * heads:]], axis=2)
    w_small = jnp.pad(w16[:, :, small0:small0 + 2 * heads], ((0, 0), (0, 0), (0, LANES - 2 * heads)))
    w_proj_a, w_proj_b, w_out = (w.astype(BF16) for w in (w_proj_a, w_proj_b, w_out))

    yp, ys = x_prompt, x_sample
    outs_p, outs_s = [], []
    zero_buf = jnp.zeros((bp, CONV_WIDTH - 1, 3 * heads * hd), x_prompt.dtype)
    zero_state = jnp.zeros((bp, heads, hd, hd), state_delta.dtype)
    for l in range(depth):
        conv_w8 = jnp.pad(conv_w[l], ((0, SUBLANES - CONV_WIDTH), (0, 0)))
        p = (norm_w[l], w_main, w_small, conv_w8, gdn_a_log[l], gdn_dt_bias[l], gdn_norm_w[l],
             q_norm_w[l], k_norm_w[l], w_proj_a, w_proj_b, w_out)
        yp, *rest = _layer(yp, mod[l, :bp], zero_buf, zero_state, None, None, p, bias_tab[l], heads, True, l)
        outs_p.append(rest)
        ys, *rest = _layer(ys, mod[l, bp:], state_conv[l], state_delta[l], cache_k[l], cache_v[l],
                           p, bias_tab[l], heads, False, l)
        outs_s.append(rest)

    stack = lambda outs, i: jnp.stack([o[i] for o in outs])
    return (yp, ys,
            stack(outs_p, 0), stack(outs_p, 1), stack(outs_p, 2), stack(outs_p, 3),
            stack(outs_s, 0), stack(outs_s, 1), stack(outs_s, 2), stack(outs_s, 3))
```
